```python
import jax, jax.numpy as jnp
from jax import lax
import numpy as np

D_MODEL = 1024
BATCH = 4
SEQ = 8192
DEPTH = 2

GRID_W = 64
CTX_LEN = 256
GLA_HEADS = 4
GLA_DK = 48
GLA_DV = 96
GLA_RANK = 16
GLA_TAU = 16.0
GLA_CHUNK = 64
SWA_HEADS = 6
SWA_KV_HEADS = 2
HEAD_DIM = 64
WINDOW = 128
Q_BLOCK = 128
ROPE_BASE = 10000.0
POOL_WINDOWS = (2, 4, 8, 16)
POOL_GROUP = 64
D_FF = 2816
CONV_W = 3
EPS = 1e-6
NEG_INF = -1e30

GLA_QK = GLA_HEADS * GLA_DK
GLA_V = GLA_HEADS * GLA_DV
SWA_Q = SWA_HEADS * HEAD_DIM
SWA_KV = SWA_KV_HEADS * HEAD_DIM
POOL_W = len(POOL_WINDOWS) * POOL_GROUP
MIX_W = GLA_V + SWA_Q + POOL_W
IN_SPLITS = (GLA_QK, GLA_QK, GLA_V, GLA_V, GLA_RANK, GLA_RANK, SWA_Q, SWA_KV, SWA_KV, POOL_W)
IN_W = GLA_QK * 2 + GLA_V * 2 + GLA_RANK * 2 + SWA_Q + SWA_KV * 2 + POOL_W

kernel_name = 'hybrid_gla_swa_pool_prefix_dit'

F32 = jnp.float32


def rms_norm(x, g):
    xf = x.astype(F32)
    y = xf * lax.rsqrt(jnp.mean(xf * xf, axis=-1, keepdims=True) + EPS)
    return (y * g.astype(F32)).astype(x.dtype)


def heads(t, n):
    return t.reshape(t.shape[:-1] + (n, t.shape[-1] // n))


def flip(t):
    return jnp.flip(t, axis=1)


def in_proj_split(p):
    offs, acc = [], 0
    for w in IN_SPLITS[:-1]:
        acc += w
        offs.append(acc)
    return jnp.split(p, offs, axis=-1)


def rope_2d_tables(n_tokens):
    rows_n = n_tokens // GRID_W
    rows = jnp.repeat(jnp.arange(rows_n), GRID_W).astype(F32)
    cols = jnp.tile(jnp.arange(GRID_W), rows_n).astype(F32)
    nf = HEAD_DIM // 4
    inv = ROPE_BASE ** (-jnp.arange(nf, dtype=F32) / nf)
    ang = jnp.concatenate([rows[:, None] * inv, cols[:, None] * inv], axis=-1)
    return jnp.cos(ang), jnp.sin(ang)


def apply_rope_2d(x, cos, sin):
    xf = x.astype(F32)
    nf = HEAD_DIM // 4
    cos = cos[:, None, :]
    sin = sin[:, None, :]
    outs = []
    for a in range(2):
        xa = xf[..., a * 2 * nf:(a + 1) * 2 * nf]
        x1, x2 = xa[..., :nf], xa[..., nf:]
        ca, sa = cos[..., a * nf:(a + 1) * nf], sin[..., a * nf:(a + 1) * nf]
        outs += [x1 * ca - x2 * sa, x2 * ca + x1 * sa]
    return jnp.concatenate(outs, axis=-1).astype(x.dtype)


def gla_log_decay(z, w_dec, b_dec):
    la = jax.nn.log_sigmoid(z.astype(F32) @ w_dec.astype(F32) + b_dec.astype(F32)) / GLA_TAU
    return heads(la, GLA_HEADS)


def gla_chunked(q, k, v, log_a, s0):
    B, T, H, DK = q.shape
    DV = v.shape[-1]
    C = GLA_CHUNK
    N = T // C
    qc = q.astype(F32).reshape(B, N, C, H, DK)
    kc = k.astype(F32).reshape(B, N, C, H, DK)
    vc = v.astype(F32).reshape(B, N, C, H, DV)
    bc = jnp.cumsum(log_a.astype(F32).reshape(B, N, C, H, DK), axis=2)
    b_last = bc[:, :, -1:]
    q_dec = qc * jnp.exp(bc)
    k_inv = kc * jnp.exp(-bc)
    k_end = kc * jnp.exp(b_last - bc)
    scores = jnp.einsum('bnihd,bnjhd->bnhij', q_dec, k_inv)
    lower = jnp.tril(jnp.ones((C, C), dtype=bool))
    scores = jnp.where(lower, scores, 0.0)
    o_intra = jnp.einsum('bnhij,bnjhv->bnihv', scores, vc)
    chunk_state = jnp.einsum('bnjhd,bnjhv->nbhdv', k_end, vc)
    chunk_decay = jnp.transpose(jnp.exp(b_last[:, :, 0]), (1, 0, 2, 3))

    def step(s, inp):
        dec, upd = inp
        return dec[..., None] * s + upd, s

    s_final, s_init = lax.scan(step, s0.astype(F32), (chunk_decay, chunk_state))
    o_inter = jnp.einsum('bnihd,nbhdv->bnihv', q_dec, s_init)
    return (o_intra + o_inter).reshape(B, T, H, DV), s_final


def gla_final_state(k, v, log_a):
    b = jnp.cumsum(log_a.astype(F32), axis=1)
    w = jnp.exp(b[:, -1:] - b)
    return jnp.einsum('bthd,bthv->bhdv', k.astype(F32) * w, v.astype(F32))


def gla_output(o, g, norm_g):
    o = rms_norm(o, norm_g)
    B, T = o.shape[:2]
    return (o.reshape(B, T, -1) * jax.nn.silu(g.astype(F32))).astype(g.dtype)


def window_attention(q, k, v, k_ctx, v_ctx, sink):
    B, S, HQ, D = q.shape
    KV = k.shape[2]
    G = HQ // KV
    N = S // Q_BLOCK
    qb = q.astype(F32).reshape(B, N, Q_BLOCK, KV, G, D)
    pad = ((0, 0), (Q_BLOCK, Q_BLOCK), (0, 0), (0, 0))
    kp = jnp.pad(k.astype(F32), pad).reshape(B, N + 2, Q_BLOCK, KV, D)
    vp = jnp.pad(v.astype(F32), pad).reshape(B, N + 2, Q_BLOCK, KV, D)
    kb = jnp.concatenate([kp[:, :-2], kp[:, 1:-1], kp[:, 2:]], axis=2)
    vb = jnp.concatenate([vp[:, :-2], vp[:, 1:-1], vp[:, 2:]], axis=2)
    q_pos = jnp.arange(S).reshape(N, Q_BLOCK)
    k_pos = (jnp.arange(N)[:, None] - 1) * Q_BLOCK + jnp.arange(3 * Q_BLOCK)[None, :]
    dist = q_pos[:, :, None] - k_pos[:, None, :]
    valid = (jnp.abs(dist) <= WINDOW) & (k_pos[:, None, :] >= 0) & (k_pos[:, None, :] < S)
    scale = D ** -0.5
    s_loc = jnp.einsum('bnqkgd,bnjkd->bnkgqj', qb, kb) * scale
    s_loc = jnp.where(valid[None, :, None, None], s_loc, NEG_INF)
    s_ctx = jnp.einsum('bnqkgd,bckd->bnkgqc', qb, k_ctx.astype(F32)) * scale
    s_sink = jnp.broadcast_to(sink.astype(F32).reshape(KV, G)[None, None, :, :, None, None],
                              (B, N, KV, G, Q_BLOCK, 1))
    p = jax.nn.softmax(jnp.concatenate([s_loc, s_ctx, s_sink], axis=-1), axis=-1)
    n_loc = 3 * Q_BLOCK
    n_ctx = k_ctx.shape[1]
    o = (jnp.einsum('bnkgqj,bnjkd->bnqkgd', p[..., :n_loc], vb)
         + jnp.einsum('bnkgqc,bckd->bnqkgd', p[..., n_loc:n_loc + n_ctx], v_ctx.astype(F32)))
    return o.reshape(B, S, HQ * D).astype(q.dtype)


def context_attention(q, k, v, sink):
    B, L, HQ, D = q.shape
    KV = k.shape[2]
    G = HQ // KV
    qg = q.astype(F32).reshape(B, L, KV, G, D)
    s = jnp.einsum('blkgd,bckd->bkglc', qg, k.astype(F32)) * D ** -0.5
    s_sink = jnp.broadcast_to(sink.astype(F32).reshape(KV, G)[None, :, :, None, None], (B, KV, G, L, 1))
    p = jax.nn.softmax(jnp.concatenate([s, s_sink], axis=-1), axis=-1)
    o = jnp.einsum('bkglc,bckd->blkgd', p[..., :L], v.astype(F32))
    return o.reshape(B, L, HQ * D).astype(q.dtype)


def multiscale_pool(u, pool_w, pool_scale):
    T = u.shape[1]
    uf = u.astype(F32)
    prefix = jnp.pad(jnp.cumsum(uf, axis=1), ((0, 0), (1, 0), (0, 0)))
    t = jnp.arange(T)
    outs = []
    for g, w in enumerate(POOL_WINDOWS):
        lo = jnp.clip(t - w // 2, 0, T)
        hi = jnp.clip(t + w // 2, 0, T)
        sl = slice(g * POOL_GROUP, (g + 1) * POOL_GROUP)
        pg = prefix[:, :, sl]
        mean = (pg[:, hi] - pg[:, lo]) / (hi - lo).astype(F32)[None, :, None]
        outs.append((mean - uf[:, :, sl]) @ pool_w[g].astype(F32))
    y = jnp.concatenate(outs, axis=-1) * pool_scale.astype(F32)
    return y.astype(u.dtype)


def conv_ffn(h, w_up, conv_w, conv_b, w_down):
    u = h @ w_up
    up = jnp.pad(u, ((0, 0), (1, 1), (0, 0)))
    u = up[:, :-2] * conv_w[0] + up[:, 1:-1] * conv_w[1] + up[:, 2:] * conv_w[2] + conv_b
    a, g = jnp.split(u, 2, axis=-1)
    return (jax.nn.silu(g) * a) @ w_down


def hybrid_layer(x, ctx, mod, modc, norm1_g, w_in, gla_w_dec, gla_b_dec, gla_norm_g, q_norm_g,
                 k_norm_g, sink_logit, pool_w, pool_scale, w_out, norm2_g, w_up, conv_w, conv_b,
                 w_down, cos, sin, update_ctx):
    sh1, sc1, g1, sh2, sc2, g2 = jnp.split(mod, 6, axis=-1)
    csh1, csc1, cg1, csh2, csc2, cg2 = jnp.split(modc, 6, axis=-1)
    h = rms_norm(x, norm1_g) * (1 + sc1) + sh1
    hc = rms_norm(ctx, norm1_g) * (1 + csc1) + csh1
    gq, gk, gv, gg, zf, zb, aq, ak, av, pu = in_proj_split(h @ w_in)
    cgq, cgk, cgv, cgg, czf, czb, caq, cak, cav, cpu = in_proj_split(hc @ w_in)

    la_f = gla_log_decay(zf, gla_w_dec[0], gla_b_dec[0])
    la_b = gla_log_decay(zb, gla_w_dec[1], gla_b_dec[1])
    cla_f = gla_log_decay(czf, gla_w_dec[0], gla_b_dec[0])
    cla_b = gla_log_decay(czb, gla_w_dec[1], gla_b_dec[1])
    q = heads(gq, GLA_HEADS) * GLA_DK ** -0.5
    k = heads(gk, GLA_HEADS)
    v = heads(gv, GLA_HEADS)
    ck = heads(cgk, GLA_HEADS)
    cv = heads(cgv, GLA_HEADS)
    if update_ctx:
        cq = heads(cgq, GLA_HEADS) * GLA_DK ** -0.5
        s0 = jnp.zeros((ctx.shape[0], GLA_HEADS, GLA_DK, GLA_DV), F32)
        oc_f, st_f = gla_chunked(cq, ck, cv, cla_f, s0)
        oc_b, st_b = gla_chunked(flip(cq), flip(ck), flip(cv), flip(cla_b), s0)
        gla_ctx = gla_output(oc_f + flip(oc_b), cgg, gla_norm_g)
    else:
        st_f = gla_final_state(ck, cv, cla_f)
        st_b = gla_final_state(flip(ck), flip(cv), flip(cla_b))
    o_f, _ = gla_chunked(q, k, v, la_f, st_f)
    o_b, _ = gla_chunked(flip(q), flip(k), flip(v), flip(la_b), st_b)
    gla_lat = gla_output(o_f + flip(o_b), gg, gla_norm_g)

    aqh = apply_rope_2d(rms_norm(heads(aq, SWA_HEADS), q_norm_g), cos, sin)
    akh = apply_rope_2d(rms_norm(heads(ak, SWA_KV_HEADS), k_norm_g), cos, sin)
    avh = heads(av, SWA_KV_HEADS)
    cakh = rms_norm(heads(cak, SWA_KV_HEADS), k_norm_g)
    cavh = heads(cav, SWA_KV_HEADS)
    swa_lat = window_attention(aqh, akh, avh, cakh, cavh, sink_logit)

    pool_lat = multiscale_pool(pu, pool_w, pool_scale)

    y = jnp.concatenate([gla_lat, swa_lat, pool_lat], axis=-1) @ w_out
    x = x + g1 * y
    h2 = rms_norm(x, norm2_g) * (1 + sc2) + sh2
    x = x + g2 * conv_ffn(h2, w_up, conv_w, conv_b, w_down)

    if update_ctx:
        swa_ctx = context_attention(rms_norm(heads(caq, SWA_HEADS), q_norm_g), cakh, cavh, sink_logit)
        pool_ctx = multiscale_pool(cpu, pool_w, pool_scale)
        yc = jnp.concatenate([gla_ctx, swa_ctx, pool_ctx], axis=-1) @ w_out
        ctx = ctx + cg1 * yc
        hc2 = rms_norm(ctx, norm2_g) * (1 + csc2) + csh2
        ctx = ctx + cg2 * conv_ffn(hc2, w_up, conv_w, conv_b, w_down)
    return x, ctx


def setup_inputs(seed: int = 0) -> dict:
    key = jax.random.key(seed)
    ks = jax.random.split(key, 24)
    D = D_MODEL
    nrm = lambda k, shape: jax.random.normal(k, shape, F32)
    return {
        'x': nrm(ks[0], (BATCH, SEQ, D)),
        'c': nrm(ks[1], (BATCH, D)),
        'ctx': nrm(ks[2], (BATCH, CTX_LEN, D)),
        'c_ctx': nrm(ks[3], (D,)),
        'w_ada': nrm(ks[4], (DEPTH, D, 6 * D)) * (0.5 * D ** -0.5),
        'b_ada': nrm(ks[5], (DEPTH, 6 * D)) * 0.02,
        'norm1_g': 1.0 + 0.02 * nrm(ks[6], (DEPTH, D)),
        'w_in': nrm(ks[7], (DEPTH, D, IN_W)) * D ** -0.5,
        'gla_w_dec': nrm(ks[8], (DEPTH, 2, GLA_RANK, GLA_QK)) * GLA_RANK ** -0.5,
        'gla_b_dec': nrm(ks[9], (DEPTH, 2, GLA_QK)) * 0.1,
        'gla_norm_g': 1.0 + 0.02 * nrm(ks[10], (DEPTH, GLA_DV)),
        'q_norm_g': 1.0 + 0.02 * nrm(ks[11], (DEPTH, HEAD_DIM)),
        'k_norm_g': 1.0 + 0.02 * nrm(ks[12], (DEPTH, HEAD_DIM)),
        'sink_logit': nrm(ks[13], (DEPTH, SWA_HEADS)),
        'pool_w': nrm(ks[14], (DEPTH, len(POOL_WINDOWS), POOL_GROUP, POOL_GROUP)) * POOL_GROUP ** -0.5,
        'pool_scale': 1.0 + 0.1 * nrm(ks[15], (DEPTH, POOL_W)),
        'w_out': nrm(ks[16], (DEPTH, MIX_W, D)) * MIX_W ** -0.5,
        'norm2_g': 1.0 + 0.02 * nrm(ks[17], (DEPTH, D)),
        'w_up': nrm(ks[18], (DEPTH, D, 2 * D_FF)) * D ** -0.5,
        'conv_w': nrm(ks[19], (DEPTH, CONV_W, 2 * D_FF)) * CONV_W ** -0.5,
        'conv_b': nrm(ks[20], (DEPTH, 2 * D_FF)) * 0.02,
        'w_down': nrm(ks[21], (DEPTH, D_FF, D)) * D_FF ** -0.5,
    }


def reference(x, c, ctx, c_ctx, w_ada, b_ada, norm1_g, w_in, gla_w_dec, gla_b_dec, gla_norm_g,
              q_norm_g, k_norm_g, sink_logit, pool_w, pool_scale, w_out, norm2_g, w_up, conv_w,
              conv_b, w_down):
    cos, sin = rope_2d_tables(x.shape[1])
    c_act = jax.nn.silu(c)
    cc_act = jax.nn.silu(c_ctx)
    for l in range(DEPTH):
        mod = (c_act @ w_ada[l] + b_ada[l])[:, None, :]
        modc = cc_act @ w_ada[l] + b_ada[l]
        x, ctx = hybrid_layer(x, ctx, mod, modc, norm1_g[l], w_in[l], gla_w_dec[l], gla_b_dec[l],
                              gla_norm_g[l], q_norm_g[l], k_norm_g[l], sink_logit[l], pool_w[l],
                              pool_scale[l], w_out[l], norm2_g[l], w_up[l], conv_w[l], conv_b[l],
                              w_down[l], cos, sin, l < DEPTH - 1)
    return x
```

```python
import functools

import jax
import jax.numpy as jnp
import numpy as np
from jax import lax
from jax.experimental import pallas as pl
from jax.experimental.pallas import tpu as pltpu

F32 = jnp.float32
BF16 = jnp.bfloat16

D_MODEL = 1024
DEPTH = 2
GRID_W = 64
GLA_HEADS = 4
GLA_DK = 48
GLA_DKP = 64
GLA_DV = 96
GLA_RANK = 16
GLA_TAU = 16.0
GLA_CHUNK = 64
SWA_HEADS = 6
SWA_KV_HEADS = 2
SWA_GROUP = SWA_HEADS // SWA_KV_HEADS
HEAD_DIM = 64
WINDOW = 128
Q_BLOCK = 128
ROPE_BASE = 10000.0
POOL_WINDOWS = (2, 4, 8, 16)
POOL_GROUP = 64
POOL_HALF_MAX = max(POOL_WINDOWS) // 2
D_FF = 2816
EPS = 1e-6
NEG_INF = -1e30

GLA_QKP = GLA_HEADS * GLA_DKP
GLA_V = GLA_HEADS * GLA_DV
SWA_Q = SWA_HEADS * HEAD_DIM
SWA_KV = SWA_KV_HEADS * HEAD_DIM
POOL_W = len(POOL_WINDOWS) * POOL_GROUP
Z_PAD = 128

OFF_Q = 0
OFF_K = OFF_Q + GLA_QKP
OFF_GV = OFF_K + GLA_QKP
OFF_GG = OFF_GV + GLA_V
OFF_Z = OFF_GG + GLA_V
OFF_SV = OFF_Z + Z_PAD
OFF_SQ = OFF_SV + SWA_KV
OFF_SK = OFF_SQ + SWA_Q
OFF_PU = OFF_SK + SWA_KV
IN_WP = OFF_PU + POOL_W

SWA_HEAD_ORDER = tuple(h for t in range(SWA_GROUP) for h in (t, t + SWA_GROUP))

FF_CHUNK = 256
N_FF_CHUNKS = D_FF // FF_CHUNK
FFN_HALO = 16
V7X_VMEM_LIMIT = 56 * 1024 * 1024

_NT = (((1,), (1,)), ((), ()))
_TN = (((0,), (0,)), ((), ()))


def _dot(a, b):
    return jnp.dot(a, b, preferred_element_type=F32)


def _params(n_grid):
    return pltpu.CompilerParams(dimension_semantics=("arbitrary",) * n_grid,
                                vmem_limit_bytes=V7X_VMEM_LIMIT)


def _sigmoid(x):
    return 1.0 / (1.0 + jnp.exp(-x))


def _log_sigmoid(x):
    return jnp.minimum(x, 0.0) - jnp.log1p(jnp.exp(-jnp.abs(x)))


def _rms(x):
    return x * lax.rsqrt(jnp.mean(x * x, axis=-1, keepdims=True) + EPS)


def _mod_kernel(c_ref, w_ref, b_ref, o_ref):
    c = c_ref[...]
    act = (c * _sigmoid(c)).astype(BF16)
    o_ref[0] = _dot(act, w_ref[0].astype(BF16)) + b_ref[0]


def _modulation(c_rows, w_ada, b_ada):
    tn = 512
    n = w_ada.shape[-1]
    return pl.pallas_call(
        _mod_kernel,
        grid=(DEPTH, n // tn),
        in_specs=[pl.BlockSpec((8, D_MODEL), lambda l, j: (0, 0)),
                  pl.BlockSpec((1, D_MODEL, tn), lambda l, j: (l, 0, j)),
                  pl.BlockSpec((1, 1, tn), lambda l, j: (l, 0, j))],
        out_specs=pl.BlockSpec((1, 8, tn), lambda l, j: (l, 0, j)),
        out_shape=jax.ShapeDtypeStruct((DEPTH, 8, n), F32),
        compiler_params=_params(2),
        name="adaln_mod",
    )(c_rows, w_ada, b_ada.reshape(DEPTH, 1, n))


def _in_proj_kernel(x_ref, mod_ref, g1_ref, w_ref, wdec_ref, bdec_ref, tri_ref, ones_ref, qkg_ref,
                    cos_ref, sin_ref,
                    qdf_ref, kif_ref, kef_ref, qdb_ref, kib_ref, keb_ref, decf_ref, decb_ref,
                    gv_ref, gg_ref, sq_ref, sk_ref, sv_ref, pu_ref, *, tm, rope):
    x = x_ref[0]
    sh1 = mod_ref[0, 0:1, :]
    sc1 = mod_ref[0, 1:2, :]
    h = (_rms(x) * g1_ref[...]) * (1.0 + sc1) + sh1
    hb = h.astype(BF16)

    qk = _dot(hb, w_ref[:, OFF_Q:OFF_GV])
    q = qk[:, :GLA_QKP] * (GLA_DK ** -0.5)
    k = qk[:, GLA_QKP:]
    zsv = _dot(hb, w_ref[:, OFF_Z:OFF_SQ])
    z = zsv[:, :Z_PAD]
    sv_ref[0] = zsv[:, Z_PAD:].astype(BF16)
    la = _log_sigmoid(_dot(z.astype(BF16), wdec_ref[...]) + bdec_ref[...]) * (1.0 / GLA_TAU)
    la_hi = la.astype(BF16)
    la_lo = (la - la_hi.astype(F32)).astype(BF16)
    lower = tri_ref[0]
    upper = tri_ref[1]
    C = GLA_CHUNK
    for c in range(tm // C):
        r = slice(c * C, (c + 1) * C)
        bcf = _dot(lower, la_hi[r, :GLA_QKP]) + _dot(lower, la_lo[r, :GLA_QKP])
        bcb = _dot(upper, la_hi[r, GLA_QKP:]) + _dot(upper, la_lo[r, GLA_QKP:])
        blf = bcf[C - 1:C, :]
        blb = bcb[0:1, :]
        qc = q[r]
        kc = k[r]
        qdf_ref[0, r, :] = (qc * jnp.exp(bcf)).astype(BF16)
        kif_ref[0, r, :] = (kc * jnp.exp(-bcf)).astype(BF16)
        kef_ref[0, r, :] = (kc * jnp.exp(blf - bcf)).astype(BF16)
        qdb_ref[0, r, :] = (qc * jnp.exp(bcb)).astype(BF16)
        kib_ref[0, r, :] = (kc * jnp.exp(-bcb)).astype(BF16)
        keb_ref[0, r, :] = (kc * jnp.exp(blb - bcb)).astype(BF16)
        decf_ref[0, c] = jnp.exp(blf)
        decb_ref[0, c] = jnp.exp(blb)

    vg = _dot(hb, w_ref[:, OFF_GV:OFF_Z])
    gv_ref[0] = vg[:, :GLA_V].astype(BF16)
    gg_ref[0] = vg[:, GLA_V:].astype(BF16)

    sqk = _dot(hb, w_ref[:, OFF_SQ:OFF_PU])
    ss = _dot((sqk * sqk).astype(BF16), ones_ref[...])
    sqk = sqk * lax.rsqrt(ss * (1.0 / HEAD_DIM) + EPS) * qkg_ref[...]
    tiles = []
    for t in range((SWA_Q + SWA_KV) // 128):
        xt = sqk[:, t * 128:(t + 1) * 128]
        if rope:
            lane = lax.broadcasted_iota(jnp.int32, xt.shape, 1)
            first = (lane % 32) < 16
            partner = jnp.where(first, pltpu.roll(xt, 128 - 16, 1), pltpu.roll(xt, 16, 1))
            xt = xt * cos_ref[...] + partner * sin_ref[...]
        tiles.append(xt.astype(BF16))
    for t in range(SWA_Q // 128):
        sq_ref[0, :, t * 128:(t + 1) * 128] = tiles[t]
    sk_ref[0] = tiles[-1]

    pu_ref[0] = _dot(hb, w_ref[:, OFF_PU:IN_WP])


def _in_proj(x, mod, mod_row, g1, w, wdec, bdec, tri, ones64, qkg, cos, sin, *, tm, rope):
    B, T, D = x.shape
    nt = T // tm
    nc = T // GLA_CHUNK

    def tok(width, dtype):
        return (jax.ShapeDtypeStruct((B, T, width), dtype),
                pl.BlockSpec((1, tm, width), lambda i, b: (b, i, 0)))

    def const(arr):
        nd = arr.ndim
        return pl.BlockSpec(arr.shape, lambda i, b: (0,) * nd)

    dec = (jax.ShapeDtypeStruct((B, nc, 1, GLA_QKP), F32),
           pl.BlockSpec((1, tm // GLA_CHUNK, 1, GLA_QKP), lambda i, b: (b, i, 0, 0)))
    outs = [tok(GLA_QKP, BF16)] * 6 + [dec, dec] + [tok(GLA_V, BF16), tok(GLA_V, BF16),
                                                   tok(SWA_Q, BF16), tok(SWA_KV, BF16),
                                                   tok(SWA_KV, BF16), tok(POOL_W, F32)]
    return pl.pallas_call(
        functools.partial(_in_proj_kernel, tm=tm, rope=rope),
        grid=(nt, B),
        in_specs=[pl.BlockSpec((1, tm, D), lambda i, b: (b, i, 0)),
                  pl.BlockSpec((1, 6, D), lambda i, b: (mod_row(b), 0, 0)),
                  const(g1), const(w), const(wdec), const(bdec), const(tri), const(ones64), const(qkg),
                  pl.BlockSpec((tm, 128), lambda i, b: (i, 0)),
                  pl.BlockSpec((tm, 128), lambda i, b: (i, 0))],
        out_specs=[o[1] for o in outs],
        out_shape=[o[0] for o in outs],
        compiler_params=_params(2),
        name="in_proj",
    )(x, mod, g1, w, wdec, bdec, tri, ones64, qkg, cos, sin)


def _gla_kernel(qdf_ref, kif_ref, kef_ref, vf_ref, decf_ref, qdb_ref, kib_ref, keb_ref, vb_ref, decb_ref,
                s0_ref, bd_ref, of_ref, ob_ref, sfin_ref, st_ref, *, tg):
    i = pl.program_id(1)
    nb = pl.num_programs(1)
    C = GLA_CHUNK
    ncb = tg // C

    @pl.when(i == 0)
    def _():
        st_ref[...] = s0_ref[0]

    col = lax.broadcasted_iota(jnp.int32, (C, GLA_QKP), 1) // GLA_DKP
    vlane = lax.broadcasted_iota(jnp.int32, (C, GLA_V), 1)
    vhead = sum((vlane >= h * GLA_DV).astype(jnp.int32) for h in range(1, GLA_HEADS))
    row_i = lax.broadcasted_iota(jnp.int32, (GLA_HEADS * C, C), 0) % C
    col_j = lax.broadcasted_iota(jnp.int32, (GLA_HEADS * C, C), 1)
    bd = bd_ref[...]

    def chunk(d, qd_ref, ki_ref, ke_ref, v_ref, dec_ref, o_ref, c):
        r = slice(c * C, (c + 1) * C)
        qd = qd_ref[0, r, :]
        ki = ki_ref[0, r, :]
        ke = ke_ref[0, r, :]
        vv = v_ref[0, r, :]
        zero = jnp.zeros_like(qd)
        qs = jnp.concatenate([jnp.where(col == h, qd, zero) for h in range(GLA_HEADS)], axis=0)
        s_all = lax.dot_general(qs, ki, _NT, preferred_element_type=F32)
        keep = (col_j <= row_i) if d == 0 else (col_j >= row_i)
        p = jnp.where(keep, s_all, 0.0).astype(BF16)
        o_all = _dot(p, vv)
        st = st_ref[d]
        o = lax.dot_general(qd, st.astype(BF16), _NT, preferred_element_type=F32)
        for h in range(GLA_HEADS):
            o = o + jnp.where(vhead == h, o_all[h * C:(h + 1) * C], 0.0)
        o_ref[0, r, :] = o.astype(BF16)
        upd = lax.dot_general(vv, ke, _TN, preferred_element_type=F32)
        st_ref[d] = st * dec_ref[0, c] + upd * bd

    for c in range(ncb):
        chunk(0, qdf_ref, kif_ref, kef_ref, vf_ref, decf_ref, of_ref, c)
        chunk(1, qdb_ref, kib_ref, keb_ref, vb_ref, decb_ref, ob_ref, ncb - 1 - c)

    @pl.when(i == nb - 1)
    def _():
        sfin_ref[0] = st_ref[...]


def _gla(qdf, kif, kef, qdb, kib, keb, decf, decb, v, s0, bd, *, tg):
    B, T, _ = v.shape
    nb = T // tg
    ncb = tg // GLA_CHUNK

    def fwd(width):
        return pl.BlockSpec((1, tg, width), lambda b, i: (b, i, 0))

    def bwd(width):
        return pl.BlockSpec((1, tg, width), lambda b, i: (b, nb - 1 - i, 0))

    dec_f = pl.BlockSpec((1, ncb, 1, GLA_QKP), lambda b, i: (b, i, 0, 0))
    dec_b = pl.BlockSpec((1, ncb, 1, GLA_QKP), lambda b, i: (b, nb - 1 - i, 0, 0))
    st_spec = pl.BlockSpec((1, 2, GLA_V, GLA_QKP), lambda b, i: (b, 0, 0, 0))
    return pl.pallas_call(
        functools.partial(_gla_kernel, tg=tg),
        grid=(B, nb),
        in_specs=[fwd(GLA_QKP), fwd(GLA_QKP), fwd(GLA_QKP), fwd(GLA_V), dec_f,
                  bwd(GLA_QKP), bwd(GLA_QKP), bwd(GLA_QKP), bwd(GLA_V), dec_b,
                  st_spec, pl.BlockSpec(bd.shape, lambda b, i: (0, 0))],
        out_specs=[fwd(GLA_V), bwd(GLA_V), st_spec],
        out_shape=[jax.ShapeDtypeStruct((B, T, GLA_V), BF16),
                   jax.ShapeDtypeStruct((B, T, GLA_V), BF16),
                   jax.ShapeDtypeStruct((B, 2, GLA_V, GLA_QKP), F32)],
        scratch_shapes=[pltpu.VMEM((2, GLA_V, GLA_QKP), F32)],
        compiler_params=_params(2),
        name="gla",
    )(qdf, kif, kef, v, decf, qdb, kib, keb, v, decb, s0, bd)


def _swa_kernel(*refs, n_local, seq_len):
    if n_local:
        q_ref, kp_ref, kc_ref, kn_ref, vp_ref, vc_ref, vn_ref, ck_ref, cv_ref, sink_ref, o_ref = refs
        k_parts = [kp_ref[0], kc_ref[0], kn_ref[0], ck_ref[0]]
        v_parts = [vp_ref[0], vc_ref[0], vn_ref[0], cv_ref[0]]
    else:
        q_ref, ck_ref, cv_ref, sink_ref, o_ref = refs
        k_parts = [ck_ref[0]]
        v_parts = [cv_ref[0]]
    n = pl.program_id(1)
    QB = Q_BLOCK
    kcat = jnp.concatenate(k_parts, axis=0)
    vcat = jnp.concatenate(v_parts, axis=0)

    lane = lax.broadcasted_iota(jnp.int32, (QB, 128), 1)
    low = lane < HEAD_DIM
    slabs = []
    for g in range(SWA_KV_HEADS):
        for t in range(SWA_GROUP):
            qt = q_ref[0, :, t * 128:(t + 1) * 128]
            slabs.append(jnp.where(low if g == 0 else ~low, qt, jnp.zeros_like(qt)))
    q_all = jnp.concatenate(slabs, axis=0)
    s_all = lax.dot_general(q_all, kcat, _NT, preferred_element_type=F32)

    n_loc = n_local * QB
    if n_local:
        qi = lax.broadcasted_iota(jnp.int32, (QB, n_loc), 0)
        kj = lax.broadcasted_iota(jnp.int32, (QB, n_loc), 1)
        k_pos = (n - 1) * QB + kj
        dist = kj - QB - qi
        valid = (dist >= -WINDOW) & (dist <= WINDOW) & (k_pos >= 0) & (k_pos < seq_len)

    ps, dens = [], []
    for s in range(SWA_HEADS):
        head = (s % SWA_GROUP) + (s // SWA_GROUP) * SWA_GROUP
        sink = sink_ref[head]
        sh = s_all[s * QB:(s + 1) * QB]
        s_ctx = sh[:, n_loc:]
        m = jnp.maximum(jnp.max(s_ctx, axis=-1, keepdims=True), sink)
        if n_local:
            s_loc = jnp.where(valid, sh[:, :n_loc], NEG_INF)
            m = jnp.maximum(m, jnp.max(s_loc, axis=-1, keepdims=True))
            p_loc = jnp.exp(s_loc - m)
        p_ctx = jnp.exp(s_ctx - m)
        den = jnp.sum(p_ctx, axis=-1, keepdims=True) + jnp.exp(sink - m)
        if n_local:
            den = den + jnp.sum(p_loc, axis=-1, keepdims=True)
            ps.append(jnp.concatenate([p_loc, p_ctx], axis=1).astype(BF16))
        else:
            ps.append(p_ctx.astype(BF16))
        dens.append(den)
    o_all = _dot(jnp.concatenate(ps, axis=0), vcat)
    for t in range(SWA_GROUP):
        a = o_all[t * QB:(t + 1) * QB] / dens[t]
        b = o_all[(SWA_GROUP + t) * QB:(SWA_GROUP + t + 1) * QB] / dens[SWA_GROUP + t]
        o_ref[0, :, t * 128:(t + 1) * 128] = jnp.where(low, a, b).astype(BF16)


def _swa(q, k, v, ck, cv, sink, *, local):
    B, S, _ = q.shape
    L = ck.shape[1]
    nq = S // Q_BLOCK
    qspec = pl.BlockSpec((1, Q_BLOCK, SWA_Q), lambda b, n: (b, n, 0))
    cspec = pl.BlockSpec((1, L, SWA_KV), lambda b, n: (b, 0, 0))
    sspec = pl.BlockSpec(memory_space=pltpu.SMEM)
    if local:
        def kv(off):
            return pl.BlockSpec((1, Q_BLOCK, SWA_KV),
                                lambda b, n: (b, jnp.clip(n + off, 0, nq - 1), 0))
        in_specs = [qspec, kv(-1), kv(0), kv(1), kv(-1), kv(0), kv(1), cspec, cspec, sspec]
        args = (q, k, k, k, v, v, v, ck, cv, sink)
    else:
        in_specs = [qspec, cspec, cspec, sspec]
        args = (q, ck, cv, sink)
    return pl.pallas_call(
        functools.partial(_swa_kernel, n_local=3 if local else 0, seq_len=S),
        grid=(B, nq),
        in_specs=in_specs,
        out_specs=qspec,
        out_shape=jax.ShapeDtypeStruct((B, S, SWA_Q), BF16),
        compiler_params=_params(2),
        name="swa",
    )(*args)


def _out_kernel(x_ref, of_ref, ob_ref, gg_ref, swa_ref, pu_ref, pup_ref, pun_ref, mod_ref,
                gng_ref, ones_ref, poolw_ref, pools_ref, wout_ref, g2_ref,
                x1_ref, h2_ref, ext_ref, *, tm, seq_len):
    i = pl.program_id(0)
    nt = pl.num_programs(0)
    HP = POOL_HALF_MAX

    o = of_ref[0].astype(F32) + ob_ref[0].astype(F32)
    ss = _dot((o * o).astype(BF16), ones_ref[...])
    gg = gg_ref[0].astype(F32)
    gla = o * lax.rsqrt(ss * (1.0 / GLA_DV) + EPS) * gng_ref[...] * (gg * _sigmoid(gg))

    ext_ref[0:HP, :] = jnp.where(i > 0, pup_ref[0], 0.0)
    ext_ref[HP:HP + tm, :] = pu_ref[0]
    ext_ref[HP + tm:, :] = jnp.where(i < nt - 1, pun_ref[0], 0.0)
    lane = lax.broadcasted_iota(jnp.int32, (tm, POOL_W), 1)
    t_pos = i * tm + lax.broadcasted_iota(jnp.int32, (tm, POOL_W), 0)
    half = jnp.full((tm, POOL_W), POOL_WINDOWS[0] // 2, jnp.int32)
    for g in range(1, len(POOL_WINDOWS)):
        half = jnp.where(lane >= g * POOL_GROUP, POOL_WINDOWS[g] // 2, half)
    cnt = (jnp.minimum(t_pos + half, seq_len) - jnp.maximum(t_pos - half, 0)).astype(F32)
    acc = jnp.zeros((tm, POOL_W), F32)
    win = jnp.zeros((tm, POOL_W), F32)
    done = 0
    for g, w in enumerate(POOL_WINDOWS):
        hw = w // 2
        for d in list(range(-hw, -done)) + list(range(done, hw)):
            acc = acc + ext_ref[HP + d:HP + d + tm, :]
        done = hw
        win = jnp.where((lane >= g * POOL_GROUP) & (lane < (g + 1) * POOL_GROUP), acc, win)
    centred = win / cnt - pu_ref[0]
    pool = _dot(centred.astype(BF16), poolw_ref[...]) * pools_ref[...]

    cat = jnp.concatenate([gla.astype(BF16), swa_ref[0], pool.astype(BF16)], axis=1)
    y = _dot(cat, wout_ref[...])
    g1 = mod_ref[0, 2:3, :]
    sh2 = mod_ref[0, 3:4, :]
    sc2 = mod_ref[0, 4:5, :]
    x1 = x_ref[0] + g1 * y
    x1_ref[0] = x1
    h2_ref[0] = ((_rms(x1) * g2_ref[...]) * (1.0 + sc2) + sh2).astype(BF16)


def _out_proj(x, o_f, o_b, gg, swa, pu, mod, mod_row, gng, ones96, poolw, pools, wout, g2, *, tm):
    B, T, D = x.shape
    nt = T // tm
    HP = POOL_HALF_MAX
    nh = T // HP

    def tok(width):
        return pl.BlockSpec((1, tm, width), lambda i, b: (b, i, 0))

    def const(arr):
        nd = arr.ndim
        return pl.BlockSpec(arr.shape, lambda i, b: (0,) * nd)

    prev = pl.BlockSpec((1, HP, POOL_W), lambda i, b: (b, jnp.maximum(i * (tm // HP) - 1, 0), 0))
    nxt = pl.BlockSpec((1, HP, POOL_W), lambda i, b: (b, jnp.minimum((i + 1) * (tm // HP), nh - 1), 0))
    return pl.pallas_call(
        functools.partial(_out_kernel, tm=tm, seq_len=T),
        grid=(nt, B),
        in_specs=[tok(D), tok(GLA_V), tok(GLA_V), tok(GLA_V), tok(SWA_Q), tok(POOL_W), prev, nxt,
                  pl.BlockSpec((1, 6, D), lambda i, b: (mod_row(b), 0, 0)),
                  const(gng), const(ones96), const(poolw), const(pools), const(wout), const(g2)],
        out_specs=[tok(D), tok(D)],
        out_shape=[jax.ShapeDtypeStruct((B, T, D), F32), jax.ShapeDtypeStruct((B, T, D), BF16)],
        scratch_shapes=[pltpu.VMEM((tm + 2 * HP, POOL_W), F32)],
        compiler_params=_params(2),
        name="out_proj",
    )(x, o_f, o_b, gg, swa, pu, pu, pu, mod, gng, ones96, poolw, pools, wout, g2)


def _ffn_kernel(h_ref, hp_ref, hn_ref, x_ref, mod_ref, wup_ref, cw_ref, cb_ref, wdn_ref,
                o_ref, u_ref, act_ref, *, tm):
    i = pl.program_id(0)
    nt = pl.num_programs(0)
    H = FFN_HALO
    h = h_ref[0]
    hp = jnp.where(i > 0, hp_ref[0], jnp.zeros_like(hp_ref[0]))
    hn = jnp.where(i < nt - 1, hn_ref[0], jnp.zeros_like(hn_ref[0]))
    for fc in range(N_FF_CHUNKS):
        for part in range(2):
            cols = slice(part * D_FF + fc * FF_CHUNK, part * D_FF + (fc + 1) * FF_CHUNK)
            w = wup_ref[:, cols]
            u_ref[part, 0:H, :] = _dot(hp, w)
            u_ref[part, H:H + tm, :] = _dot(h, w)
            u_ref[part, H + tm:, :] = _dot(hn, w)
        conv = []
        for part in range(2):
            cols = slice(part * D_FF + fc * FF_CHUNK, part * D_FF + (fc + 1) * FF_CHUNK)
            conv.append(u_ref[part, H - 1:H - 1 + tm, :] * cw_ref[0:1, cols]
                        + u_ref[part, H:H + tm, :] * cw_ref[1:2, cols]
                        + u_ref[part, H + 1:H + 1 + tm, :] * cw_ref[2:3, cols]
                        + cb_ref[0:1, cols])
        a, g = conv
        act_ref[:, fc * FF_CHUNK:(fc + 1) * FF_CHUNK] = ((g * _sigmoid(g)) * a).astype(BF16)
    y = _dot(act_ref[...], wdn_ref[...])
    o_ref[0] = x_ref[0] + mod_ref[0, 5:6, :] * y


def _ffn(h2, x1, mod, mod_row, wup, cw, cb, wdn, *, tm):
    B, T, D = x1.shape
    nt = T // tm
    H = FFN_HALO
    nh = T // H

    def tok(width):
        return pl.BlockSpec((1, tm, width), lambda i, b: (b, i, 0))

    def const(arr):
        nd = arr.ndim
        return pl.BlockSpec(arr.shape, lambda i, b: (0,) * nd, pipeline_mode=pl.Buffered(1))

    prev = pl.BlockSpec((1, H, D), lambda i, b: (b, jnp.maximum(i * (tm // H) - 1, 0), 0))
    nxt = pl.BlockSpec((1, H, D), lambda i, b: (b, jnp.minimum((i + 1) * (tm // H), nh - 1), 0))
    return pl.pallas_call(
        functools.partial(_ffn_kernel, tm=tm),
        grid=(nt, B),
        in_specs=[tok(D), prev, nxt, tok(D),
                  pl.BlockSpec((1, 6, D), lambda i, b: (mod_row(b), 0, 0)),
                  const(wup), const(cw), const(cb), const(wdn)],
        out_specs=tok(D),
        out_shape=jax.ShapeDtypeStruct((B, T, D), F32),
        scratch_shapes=[pltpu.VMEM((2, tm + 2 * H, FF_CHUNK), F32), pltpu.VMEM((tm, D_FF), BF16)],
        compiler_params=_params(2),
        name="ffn",
    )(h2, h2, h2, x1, mod, wup, cw, cb, wdn)


def _pad_heads(a, n_heads, width, padded):
    a = a.reshape(a.shape[:-1] + (n_heads, width))
    a = jnp.pad(a, [(0, 0)] * (a.ndim - 1) + [(0, padded - width)])
    return a.reshape(a.shape[:-2] + (n_heads * padded,))


def _permute_heads(a, axis):
    shape = a.shape
    a = a.reshape(shape[:axis] + (SWA_HEADS, HEAD_DIM) + shape[axis + 1:])
    a = jnp.take(a, jnp.array(SWA_HEAD_ORDER), axis=axis)
    return a.reshape(shape)


def _block_diag_ones(n, block):
    idx = np.arange(n) // block
    return jnp.asarray(idx[:, None] == idx[None, :], BF16)


def _pack_layer(w_in, gla_w_dec, gla_b_dec, q_norm_g, k_norm_g, pool_w, w_out, conv_w, conv_b):
    gq, gk, gv, gg, zf, zb, aq, ak, av, pu = jnp.split(
        w_in, np.cumsum([192, 192, 384, 384, 16, 16, 384, 128, 128])[:].tolist(), axis=-1)
    z = jnp.pad(jnp.concatenate([zf, zb], axis=-1), ((0, 0), (0, Z_PAD - 2 * GLA_RANK)))
    w = jnp.concatenate([_pad_heads(gq, GLA_HEADS, GLA_DK, GLA_DKP),
                         _pad_heads(gk, GLA_HEADS, GLA_DK, GLA_DKP),
                         gv, gg, z, av, _permute_heads(aq, 1), ak, pu], axis=-1).astype(BF16)
    wdec = jnp.zeros((Z_PAD, 2 * GLA_QKP), F32)
    wdec = wdec.at[0:GLA_RANK, :GLA_QKP].set(_pad_heads(gla_w_dec[0], GLA_HEADS, GLA_DK, GLA_DKP))
    wdec = wdec.at[GLA_RANK:2 * GLA_RANK, GLA_QKP:].set(_pad_heads(gla_w_dec[1], GLA_HEADS, GLA_DK, GLA_DKP))
    bdec = jnp.concatenate([_pad_heads(gla_b_dec[0], GLA_HEADS, GLA_DK, GLA_DKP),
                            _pad_heads(gla_b_dec[1], GLA_HEADS, GLA_DK, GLA_DKP)])[None, :]
    qkg = jnp.concatenate([jnp.tile(q_norm_g, SWA_HEADS) * (HEAD_DIM ** -0.5),
                           jnp.tile(k_norm_g, SWA_KV_HEADS)])[None, :]
    poolw = jax.scipy.linalg.block_diag(*[pool_w[g] for g in range(len(POOL_WINDOWS))]).astype(BF16)
    wout = jnp.concatenate([w_out[:GLA_V], _permute_heads(w_out[GLA_V:GLA_V + SWA_Q], 0),
                            w_out[GLA_V + SWA_Q:]], axis=0).astype(BF16)
    return dict(w=w, wdec=wdec.astype(BF16), bdec=bdec, qkg=qkg, poolw=poolw, wout=wout,
                cw=conv_w, cb=conv_b[None, :])


def _rope_tables(n_tokens):
    rows_n = n_tokens // GRID_W
    rows = jnp.repeat(jnp.arange(rows_n), GRID_W).astype(F32)
    cols = jnp.tile(jnp.arange(GRID_W), rows_n).astype(F32)
    nf = HEAD_DIM // 4
    inv = ROPE_BASE ** (-jnp.arange(nf, dtype=F32) / nf)
    ar = rows[:, None] * inv
    ac = cols[:, None] * inv
    cos = jnp.concatenate([jnp.cos(ar), jnp.cos(ar), jnp.cos(ac), jnp.cos(ac)], axis=-1)
    sin = jnp.concatenate([-jnp.sin(ar), jnp.sin(ar), -jnp.sin(ac), jnp.sin(ac)], axis=-1)
    return jnp.tile(cos, (1, 128 // HEAD_DIM)), jnp.tile(sin, (1, 128 // HEAD_DIM))


def _tile_sizes(T):
    return dict(tm=min(T, 512), tg=min(T, 256))


def kernel(x, c, ctx, c_ctx, w_ada, b_ada, norm1_g, w_in, gla_w_dec, gla_b_dec, gla_norm_g, q_norm_g,
           k_norm_g, sink_logit, pool_w, pool_scale, w_out, norm2_g, w_up, conv_w, conv_b, w_down):
    B, S, D = x.shape
    L = ctx.shape[1]
    c_rows = jnp.zeros((8, D), F32).at[:B].set(c).at[B].set(c_ctx)
    mod_all = _modulation(c_rows, w_ada, b_ada).reshape(DEPTH, 8, 6, D)
    cos, sin = _rope_tables(S)
    tri = jnp.stack([jnp.tril(jnp.ones((GLA_CHUNK, GLA_CHUNK), BF16)),
                     jnp.triu(jnp.ones((GLA_CHUNK, GLA_CHUNK), BF16))])
    ones64 = _block_diag_ones(SWA_Q + SWA_KV, HEAD_DIM)
    ones96 = _block_diag_ones(GLA_V, GLA_DV)
    bd = jnp.asarray((np.arange(GLA_V) // GLA_DV)[:, None] == (np.arange(GLA_QKP) // GLA_DKP)[None, :], F32)
    lat, cx = _tile_sizes(S), _tile_sizes(L)
    lat_row = lambda b: b
    ctx_row = lambda b: B
    zero_state = jnp.zeros((B, 2, GLA_V, GLA_QKP), F32)

    for l in range(DEPTH):
        p = _pack_layer(w_in[l], gla_w_dec[l], gla_b_dec[l], q_norm_g[l], k_norm_g[l], pool_w[l],
                        w_out[l], conv_w[l], conv_b[l])
        mod = mod_all[l]
        g1 = norm1_g[l][None, :]
        g2 = norm2_g[l][None, :]
        gng = jnp.tile(gla_norm_g[l], GLA_HEADS)[None, :]
        pools = pool_scale[l][None, :]
        wup = w_up[l].astype(BF16)
        wdn = w_down[l].astype(BF16)
        shared = (g1, p["w"], p["wdec"], p["bdec"], tri, ones64, p["qkg"])
        update_ctx = l < DEPTH - 1

        (cqdf, ckif, ckef, cqdb, ckib, ckeb, cdecf, cdecb, cgv, cgg, csq, csk, csv, cpu) = _in_proj(
            ctx, mod, ctx_row, *shared, cos[:L], sin[:L], tm=cx["tm"], rope=False)
        co_f, co_b, st = _gla(cqdf, ckif, ckef, cqdb, ckib, ckeb, cdecf, cdecb, cgv, zero_state, bd,
                              tg=cx["tg"])

        (qdf, kif, kef, qdb, kib, keb, decf, decb, gv, gg, sq, sk, sv, pu) = _in_proj(
            x, mod, lat_row, *shared, cos, sin, tm=lat["tm"], rope=True)
        o_f, o_b, _ = _gla(qdf, kif, kef, qdb, kib, keb, decf, decb, gv, st, bd, tg=lat["tg"])
        swa = _swa(sq, sk, sv, csk, csv, sink_logit[l], local=True)
        x1, h2 = _out_proj(x, o_f, o_b, gg, swa, pu, mod, lat_row, gng, ones96, p["poolw"], pools,
                           p["wout"], g2, tm=lat["tm"])
        x = _ffn(h2, x1, mod, lat_row, wup, p["cw"], p["cb"], wdn, tm=lat["tm"])

        if update_ctx:
            cswa = _swa(csq, None, None, csk, csv, sink_logit[l], local=False)
            c1, ch2 = _out_proj(ctx, co_f, co_b, cgg, cswa, cpu, mod, ctx_row, gng, ones96, p["poolw"],
                                pools, p["wout"], g2, tm=cx["tm"])
            ctx = _ffn(ch2, c1, mod, ctx_row, wup, p["cw"], p["cb"], wdn, tm=cx["tm"])
    return x
```

```python
import functools

import jax
import jax.numpy as jnp
import numpy as np
from jax import lax
from jax.experimental import pallas as pl
from jax.experimental.pallas import tpu as pltpu

F32 = jnp.float32
BF16 = jnp.bfloat16

D_MODEL = 1024
DEPTH = 2
GRID_W = 64
GLA_HEADS = 4
GLA_DK = 48
GLA_DKP = 64
GLA_DV = 96
GLA_RANK = 16
GLA_TAU = 16.0
GLA_CHUNK = 64
SWA_HEADS = 6
SWA_KV_HEADS = 2
SWA_GROUP = SWA_HEADS // SWA_KV_HEADS
HEAD_DIM = 64
WINDOW = 128
Q_BLOCK = 128
ROPE_BASE = 10000.0
POOL_WINDOWS = (2, 4, 8, 16)
POOL_GROUP = 64
POOL_HALF_MAX = max(POOL_WINDOWS) // 2
POOL_ROWS = 128
D_FF = 2816
EPS = 1e-6
NEG_INF = -1e30

GLA_QKP = GLA_HEADS * GLA_DKP
GLA_V = GLA_HEADS * GLA_DV
SWA_Q = SWA_HEADS * HEAD_DIM
SWA_KV = SWA_KV_HEADS * HEAD_DIM
POOL_W = len(POOL_WINDOWS) * POOL_GROUP
Z_PAD = 128

OFF_Q = 0
OFF_K = OFF_Q + GLA_QKP
OFF_GV = OFF_K + GLA_QKP
OFF_GG = OFF_GV + GLA_V
OFF_Z = OFF_GG + GLA_V
OFF_SV = OFF_Z + Z_PAD
OFF_SQ = OFF_SV + SWA_KV
OFF_SK = OFF_SQ + SWA_Q
OFF_PU = OFF_SK + SWA_KV
IN_WP = OFF_PU + POOL_W

SWA_HEAD_ORDER = tuple(h for t in range(SWA_GROUP) for h in (t, t + SWA_GROUP))

FF_CHUNK = 256
N_FF_CHUNKS = D_FF // FF_CHUNK
FFN_HALO = 16
FFN_ROWS = 64
V7X_VMEM_LIMIT = 56 * 1024 * 1024

_NT = (((1,), (1,)), ((), ()))
_TN = (((0,), (0,)), ((), ()))


def _dot(a, b):
    return jnp.dot(a, b, preferred_element_type=F32)


def _params(n_grid, flags=None):
    return pltpu.CompilerParams(dimension_semantics=("arbitrary",) * n_grid,
                                vmem_limit_bytes=V7X_VMEM_LIMIT, flags=flags)


def _sigmoid(x):
    return 1.0 / (1.0 + jnp.exp(-x))


def _log_sigmoid(x):
    return jnp.minimum(x, 0.0) - jnp.log1p(jnp.exp(-jnp.abs(x)))


def _rms(x):
    return x * lax.rsqrt(jnp.mean(x * x, axis=-1, keepdims=True) + EPS)


def _mod_kernel(c_ref, w_ref, b_ref, o_ref):
    c = c_ref[...]
    act = (c * _sigmoid(c)).astype(BF16)
    o_ref[0] = _dot(act, w_ref[0].astype(BF16)) + b_ref[0]


def _modulation(c_rows, w_ada, b_ada):
    tn = 512
    n = w_ada.shape[-1]
    return pl.pallas_call(
        _mod_kernel,
        grid=(DEPTH, n // tn),
        in_specs=[pl.BlockSpec((8, D_MODEL), lambda l, j: (0, 0)),
                  pl.BlockSpec((1, D_MODEL, tn), lambda l, j: (l, 0, j)),
                  pl.BlockSpec((1, 1, tn), lambda l, j: (l, 0, j))],
        out_specs=pl.BlockSpec((1, 8, tn), lambda l, j: (l, 0, j)),
        out_shape=jax.ShapeDtypeStruct((DEPTH, 8, n), F32),
        compiler_params=_params(2),
        name="adaln_mod",
    )(c_rows, w_ada, b_ada.reshape(DEPTH, 1, n))


def _in_proj_kernel(x_ref, mod_ref, g1_ref, w_ref, wdec_ref, bdec_ref, tri_ref, ones_ref, qkg_ref,
                    cos_ref, sin_ref,
                    qdf_ref, kif_ref, kef_ref, qdb_ref, kib_ref, keb_ref, decf_ref, decb_ref,
                    gv_ref, gg_ref, sq_ref, skv_ref, pu_ref, *, tm, rope):
    x = x_ref[0]
    sh1 = mod_ref[0, 0:1, :]
    sc1 = mod_ref[0, 1:2, :]
    h = _rms(x) * (g1_ref[...] * (1.0 + sc1)) + sh1
    hb = h.astype(BF16)

    qk = _dot(hb, w_ref[:, OFF_Q:OFF_GV])
    q = qk[:, :GLA_QKP] * (GLA_DK ** -0.5)
    k = qk[:, GLA_QKP:]
    zsv = _dot(hb, w_ref[:, OFF_Z:OFF_SQ])
    z = zsv[:, :Z_PAD]
    skv_ref[0, :, SWA_KV:] = zsv[:, Z_PAD:].astype(BF16)
    la = _log_sigmoid(_dot(z.astype(BF16), wdec_ref[...]) + bdec_ref[...]) * (1.0 / GLA_TAU)
    la_hi = la.astype(BF16)
    la_lo = (la - la_hi.astype(F32)).astype(BF16)
    lower = tri_ref[0]
    upper = tri_ref[1]
    C = GLA_CHUNK
    for c in range(tm // C):
        r = slice(c * C, (c + 1) * C)
        bcf = _dot(lower, la_hi[r, :GLA_QKP]) + _dot(lower, la_lo[r, :GLA_QKP])
        bcb = _dot(upper, la_hi[r, GLA_QKP:]) + _dot(upper, la_lo[r, GLA_QKP:])
        blf = bcf[C - 1:C, :]
        blb = bcb[0:1, :]
        qc = q[r]
        kc = k[r]
        qdf_ref[0, r, :] = (qc * jnp.exp(bcf)).astype(BF16)
        kif_ref[0, r, :] = (kc * jnp.exp(-bcf)).astype(BF16)
        kef_ref[0, r, :] = (kc * jnp.exp(blf - bcf)).astype(BF16)
        qdb_ref[0, r, :] = (qc * jnp.exp(bcb)).astype(BF16)
        kib_ref[0, r, :] = (kc * jnp.exp(-bcb)).astype(BF16)
        keb_ref[0, r, :] = (kc * jnp.exp(blb - bcb)).astype(BF16)
        decf_ref[0, c] = jnp.exp(blf)
        decb_ref[0, c] = jnp.exp(blb)

    vg = _dot(hb, w_ref[:, OFF_GV:OFF_Z])
    gv_ref[0] = vg[:, :GLA_V].astype(BF16)
    gg_ref[0] = vg[:, GLA_V:].astype(BF16)

    sqk = _dot(hb, w_ref[:, OFF_SQ:OFF_PU])
    ss = _dot((sqk * sqk).astype(BF16), ones_ref[...])
    sqk = sqk * lax.rsqrt(ss + EPS) * qkg_ref[...]
    tiles = []
    for t in range((SWA_Q + SWA_KV) // 128):
        xt = sqk[:, t * 128:(t + 1) * 128]
        if rope:
            lane = lax.broadcasted_iota(jnp.int32, xt.shape, 1)
            first = (lane % 32) < 16
            partner = jnp.where(first, pltpu.roll(xt, 128 - 16, 1), pltpu.roll(xt, 16, 1))
            xt = xt * cos_ref[...] + partner * sin_ref[...]
        tiles.append(xt.astype(BF16))
    for t in range(SWA_Q // 128):
        sq_ref[0, :, t * 128:(t + 1) * 128] = tiles[t]
    skv_ref[0, :, :SWA_KV] = tiles[-1]

    pu_ref[0] = _dot(hb, w_ref[:, OFF_PU:IN_WP])


def _in_proj(x, mod, mod_row, g1, w, wdec, bdec, tri, ones64, qkg, cos, sin, *, tm, rope):
    B, T, D = x.shape
    nt = T // tm
    nc = T // GLA_CHUNK

    def tok(width, dtype):
        return (jax.ShapeDtypeStruct((B, T, width), dtype),
                pl.BlockSpec((1, tm, width), lambda i, b: (b, i, 0)))

    def const(arr):
        nd = arr.ndim
        return pl.BlockSpec(arr.shape, lambda i, b: (0,) * nd)

    dec = (jax.ShapeDtypeStruct((B, nc, 1, GLA_QKP), F32),
           pl.BlockSpec((1, tm // GLA_CHUNK, 1, GLA_QKP), lambda i, b: (b, i, 0, 0)))
    outs = [tok(GLA_QKP, BF16)] * 6 + [dec, dec] + [tok(GLA_V, BF16), tok(GLA_V, BF16),
                                                   tok(SWA_Q, BF16), tok(2 * SWA_KV, BF16),
                                                   tok(POOL_W, F32)]
    return pl.pallas_call(
        functools.partial(_in_proj_kernel, tm=tm, rope=rope),
        grid=(nt, B),
        in_specs=[pl.BlockSpec((1, tm, D), lambda i, b: (b, i, 0)),
                  pl.BlockSpec((1, 6, D), lambda i, b: (mod_row(b), 0, 0)),
                  const(g1), const(w), const(wdec), const(bdec), const(tri), const(ones64), const(qkg),
                  pl.BlockSpec((tm, 128), lambda i, b: (i, 0)),
                  pl.BlockSpec((tm, 128), lambda i, b: (i, 0))],
        out_specs=[o[1] for o in outs],
        out_shape=[o[0] for o in outs],
        compiler_params=_params(2),
        name="in_proj",
    )(x, mod, g1, w, wdec, bdec, tri, ones64, qkg, cos, sin)


def _gla_kernel(qdf_ref, kif_ref, kef_ref, vf_ref, decf_ref, qdb_ref, kib_ref, keb_ref, vb_ref, decb_ref,
                s0_ref, bd_ref, of_ref, ob_ref, sfin_ref, st_ref, *, tg):
    i = pl.program_id(1)
    nb = pl.num_programs(1)
    C = GLA_CHUNK
    ncb = tg // C

    @pl.when(i == 0)
    def _():
        st_ref[...] = s0_ref[0]

    HC = GLA_HEADS * C
    row_head = lax.broadcasted_iota(jnp.int32, (HC, GLA_QKP), 0) // C
    k_mask = row_head == lax.broadcasted_iota(jnp.int32, (HC, GLA_QKP), 1) // GLA_DKP
    vlane = lax.broadcasted_iota(jnp.int32, (HC, GLA_V), 1)
    vhead = sum((vlane >= h * GLA_DV).astype(jnp.int32) for h in range(1, GLA_HEADS))
    v_mask = lax.broadcasted_iota(jnp.int32, (HC, GLA_V), 0) // C == vhead
    row_i = lax.broadcasted_iota(jnp.int32, (C, HC), 0)
    col_j = lax.broadcasted_iota(jnp.int32, (C, HC), 1) % C
    bd = bd_ref[...]

    def chunk(d, qd_ref, ki_ref, ke_ref, v_ref, dec_ref, o_ref, c):
        r = slice(c * C, (c + 1) * C)
        qd = qd_ref[0, r, :]
        ki = ki_ref[0, r, :]
        ke = ke_ref[0, r, :]
        vv = v_ref[0, r, :]
        k_bd = jnp.where(k_mask, jnp.concatenate([ki] * GLA_HEADS, axis=0), jnp.zeros((), BF16))
        v_bd = jnp.where(v_mask, jnp.concatenate([vv] * GLA_HEADS, axis=0), jnp.zeros((), BF16))
        st = st_ref[d]
        rhs = jnp.concatenate([k_bd, st.astype(BF16)], axis=0)
        res = lax.dot_general(qd, rhs, _NT, preferred_element_type=F32)
        keep = (col_j <= row_i) if d == 0 else (col_j >= row_i)
        p = jnp.where(keep, res[:, :HC], 0.0).astype(BF16)
        o = res[:, HC:] + _dot(p, v_bd)
        o_ref[0, r, :] = o.astype(BF16)
        upd = lax.dot_general(vv, ke, _TN, preferred_element_type=F32)
        st_ref[d] = st * dec_ref[0, c] + upd * bd

    for c in range(ncb):
        chunk(0, qdf_ref, kif_ref, kef_ref, vf_ref, decf_ref, of_ref, c)
        chunk(1, qdb_ref, kib_ref, keb_ref, vb_ref, decb_ref, ob_ref, ncb - 1 - c)

    @pl.when(i == nb - 1)
    def _():
        sfin_ref[0] = st_ref[...]


def _gla(qdf, kif, kef, qdb, kib, keb, decf, decb, v, s0, bd, *, tg):
    B, T, _ = v.shape
    nb = T // tg
    ncb = tg // GLA_CHUNK

    def fwd(width):
        return pl.BlockSpec((1, tg, width), lambda b, i: (b, i, 0))

    def bwd(width):
        return pl.BlockSpec((1, tg, width), lambda b, i: (b, nb - 1 - i, 0))

    dec_f = pl.BlockSpec((1, ncb, 1, GLA_QKP), lambda b, i: (b, i, 0, 0))
    dec_b = pl.BlockSpec((1, ncb, 1, GLA_QKP), lambda b, i: (b, nb - 1 - i, 0, 0))
    st_spec = pl.BlockSpec((1, 2, GLA_V, GLA_QKP), lambda b, i: (b, 0, 0, 0))
    return pl.pallas_call(
        functools.partial(_gla_kernel, tg=tg),
        grid=(B, nb),
        in_specs=[fwd(GLA_QKP), fwd(GLA_QKP), fwd(GLA_QKP), fwd(GLA_V), dec_f,
                  bwd(GLA_QKP), bwd(GLA_QKP), bwd(GLA_QKP), bwd(GLA_V), dec_b,
                  st_spec, pl.BlockSpec(bd.shape, lambda b, i: (0, 0))],
        out_specs=[fwd(GLA_V), bwd(GLA_V), st_spec],
        out_shape=[jax.ShapeDtypeStruct((B, T, GLA_V), BF16),
                   jax.ShapeDtypeStruct((B, T, GLA_V), BF16),
                   jax.ShapeDtypeStruct((B, 2, GLA_V, GLA_QKP), F32)],
        scratch_shapes=[pltpu.VMEM((2, GLA_V, GLA_QKP), F32)],
        compiler_params=_params(2),
        name="gla",
    )(qdf, kif, kef, v, decf, qdb, kib, keb, v, decb, s0, bd)


def _swa_kernel(*refs, n_local, seq_len):
    if n_local:
        q_ref, kvp_ref, kvc_ref, kvn_ref, ckv_ref, sink_ref, o_ref = refs
        kv_parts = [kvp_ref, kvc_ref, kvn_ref, ckv_ref]
    else:
        q_ref, ckv_ref, sink_ref, o_ref = refs
        kv_parts = [ckv_ref]
    n = pl.program_id(1)
    QB = Q_BLOCK
    kcat = jnp.concatenate([r[0, :, :SWA_KV] for r in kv_parts], axis=0)
    vcat = jnp.concatenate([r[0, :, SWA_KV:] for r in kv_parts], axis=0)

    lane = lax.broadcasted_iota(jnp.int32, (QB, 128), 1)
    low = lane < HEAD_DIM
    n_loc = n_local * QB
    if n_local:
        qi = lax.broadcasted_iota(jnp.int32, (QB, n_loc), 0)
        kj = lax.broadcasted_iota(jnp.int32, (QB, n_loc), 1)
        k_pos = (n - 1) * QB + kj
        dist = kj - QB - qi
        valid = (dist >= -WINDOW) & (dist <= WINDOW) & (k_pos >= 0) & (k_pos < seq_len)

    def scores(g):
        slabs = []
        for t in range(SWA_GROUP):
            qt = q_ref[0, :, t * 128:(t + 1) * 128]
            slabs.append(jnp.where(low if g == 0 else ~low, qt, jnp.zeros_like(qt)))
        return lax.dot_general(jnp.concatenate(slabs, axis=0), kcat, _NT,
                               preferred_element_type=F32)

    def attend(g, s_grp):
        ps, dens = [], []
        for t in range(SWA_GROUP):
            sink = sink_ref[g * SWA_GROUP + t]
            sh = s_grp[t * QB:(t + 1) * QB]
            s_ctx = sh[:, n_loc:]
            m = jnp.maximum(jnp.max(s_ctx, axis=-1, keepdims=True), sink)
            if n_local:
                s_loc = jnp.where(valid, sh[:, :n_loc], NEG_INF)
                m = jnp.maximum(m, jnp.max(s_loc, axis=-1, keepdims=True))
                p_loc = jnp.exp(s_loc - m)
            p_ctx = jnp.exp(s_ctx - m)
            den = jnp.sum(p_ctx, axis=-1, keepdims=True) + jnp.exp(sink - m)
            if n_local:
                den = den + jnp.sum(p_loc, axis=-1, keepdims=True)
                ps.append(jnp.concatenate([p_loc, p_ctx], axis=1).astype(BF16))
            else:
                ps.append(p_ctx.astype(BF16))
            dens.append(den)
        o_grp = _dot(jnp.concatenate(ps, axis=0), vcat)
        return [o_grp[t * QB:(t + 1) * QB] / dens[t] for t in range(SWA_GROUP)]

    s_groups = [scores(g) for g in range(SWA_KV_HEADS)]
    o_a = attend(0, s_groups[0])
    o_b = attend(1, s_groups[1])
    for t in range(SWA_GROUP):
        o_ref[0, :, t * 128:(t + 1) * 128] = jnp.where(low, o_a[t], o_b[t]).astype(BF16)


def _swa(q, kv, ckv, sink, *, local):
    B, S, _ = q.shape
    L = ckv.shape[1]
    nq = S // Q_BLOCK
    qspec = pl.BlockSpec((1, Q_BLOCK, SWA_Q), lambda b, n: (b, n, 0))
    cspec = pl.BlockSpec((1, L, 2 * SWA_KV), lambda b, n: (b, 0, 0))
    sspec = pl.BlockSpec(memory_space=pltpu.SMEM)
    if local:
        def blk(off):
            return pl.BlockSpec((1, Q_BLOCK, 2 * SWA_KV),
                                lambda b, n: (b, jnp.clip(n + off, 0, nq - 1), 0))
        in_specs = [qspec, blk(-1), blk(0), blk(1), cspec, sspec]
        args = (q, kv, kv, kv, ckv, sink)
    else:
        in_specs = [qspec, cspec, sspec]
        args = (q, ckv, sink)
    return pl.pallas_call(
        functools.partial(_swa_kernel, n_local=3 if local else 0, seq_len=S),
        grid=(B, nq),
        in_specs=in_specs,
        out_specs=qspec,
        out_shape=jax.ShapeDtypeStruct((B, S, SWA_Q), BF16),
        compiler_params=_params(2),
        name="swa",
    )(*args)


def _out_kernel(x_ref, of_ref, ob_ref, gg_ref, swa_ref, pu_ref, pup_ref, pun_ref, icnt_ref, mod_ref,
                gng_ref, ones_ref, poolw_ref, pools_ref, wout_ref, g2_ref,
                x1_ref, h2_ref, ext_ref, cen_ref, *, tm):
    i = pl.program_id(0)
    nt = pl.num_programs(0)
    HP = POOL_HALF_MAX

    o = of_ref[0].astype(F32) + ob_ref[0].astype(F32)
    ss = _dot((o * o).astype(BF16), ones_ref[...])
    gg = gg_ref[0].astype(F32)
    gla = o * lax.rsqrt(ss * (1.0 / GLA_DV) + EPS) * gng_ref[...] * (gg * _sigmoid(gg))

    ext_ref[0:HP, :] = jnp.where(i > 0, pup_ref[0], 0.0)
    ext_ref[HP:HP + tm, :] = pu_ref[0]
    ext_ref[HP + tm:, :] = jnp.where(i < nt - 1, pun_ref[0], 0.0)
    upper_group = lax.broadcasted_iota(jnp.int32, (POOL_ROWS, 128), 1) >= POOL_GROUP
    for rb in range(tm // POOL_ROWS):
        r0 = HP + rb * POOL_ROWS
        rows = slice(rb * POOL_ROWS, (rb + 1) * POOL_ROWS)
        for tile in range(POOL_W // 128):
            ls = slice(tile * 128, (tile + 1) * 128)
            acc = None
            win = None
            done = 0
            for w in POOL_WINDOWS[2 * tile:2 * tile + 2]:
                hw = w // 2
                for d in list(range(-hw, -done)) + list(range(done, hw)):
                    term = ext_ref[r0 + d:r0 + d + POOL_ROWS, ls]
                    acc = term if acc is None else acc + term
                done = hw
                win = acc if win is None else jnp.where(upper_group, acc, win)
            centred = win * icnt_ref[rows, ls] - ext_ref[r0:r0 + POOL_ROWS, ls]
            cen_ref[rows, ls] = centred.astype(BF16)
    pool = _dot(cen_ref[...], poolw_ref[...]) * pools_ref[...]

    cat = jnp.concatenate([gla.astype(BF16), swa_ref[0], pool.astype(BF16)], axis=1)
    y = _dot(cat, wout_ref[...])
    g1 = mod_ref[0, 2:3, :]
    sh2 = mod_ref[0, 3:4, :]
    sc2 = mod_ref[0, 4:5, :]
    x1 = x_ref[0] + g1 * y
    x1_ref[0] = x1
    h2_ref[0] = (_rms(x1) * (g2_ref[...] * (1.0 + sc2)) + sh2).astype(BF16)


def _out_proj(x, o_f, o_b, gg, swa, pu, icnt, mod, mod_row, gng, ones96, poolw, pools, wout, g2, *, tm):
    B, T, D = x.shape
    nt = T // tm
    HP = POOL_HALF_MAX
    nh = T // HP

    def tok(width):
        return pl.BlockSpec((1, tm, width), lambda i, b: (b, i, 0))

    def const(arr):
        nd = arr.ndim
        return pl.BlockSpec(arr.shape, lambda i, b: (0,) * nd)

    prev = pl.BlockSpec((1, HP, POOL_W), lambda i, b: (b, jnp.maximum(i * (tm // HP) - 1, 0), 0))
    nxt = pl.BlockSpec((1, HP, POOL_W), lambda i, b: (b, jnp.minimum((i + 1) * (tm // HP), nh - 1), 0))
    return pl.pallas_call(
        functools.partial(_out_kernel, tm=tm),
        grid=(nt, B),
        in_specs=[tok(D), tok(GLA_V), tok(GLA_V), tok(GLA_V), tok(SWA_Q), tok(POOL_W), prev, nxt,
                  pl.BlockSpec((tm, POOL_W), lambda i, b: (i, 0)),
                  pl.BlockSpec((1, 6, D), lambda i, b: (mod_row(b), 0, 0)),
                  const(gng), const(ones96), const(poolw), const(pools), const(wout), const(g2)],
        out_specs=[tok(D), tok(D)],
        out_shape=[jax.ShapeDtypeStruct((B, T, D), F32), jax.ShapeDtypeStruct((B, T, D), BF16)],
        scratch_shapes=[pltpu.VMEM((tm + 2 * HP, POOL_W), F32), pltpu.VMEM((tm, POOL_W), BF16)],
        compiler_params=_params(2),
        name="out_proj",
    )(x, o_f, o_b, gg, swa, pu, pu, pu, icnt, mod, gng, ones96, poolw, pools, wout, g2)


def _ffn_kernel(h_ref, hp_ref, hn_ref, x_ref, mod_ref, wup_ref, cwb_ref, wdn_ref,
                o_ref, hx_ref, u_ref, act_ref, *, tm):
    i = pl.program_id(0)
    nt = pl.num_programs(0)
    H = FFN_HALO
    N = N_FF_CHUNKS
    hx_ref[0:H, :] = jnp.where(i > 0, hp_ref[0], jnp.zeros_like(hp_ref[0]))
    hx_ref[H:H + tm, :] = h_ref[0]
    hx_ref[H + tm:, :] = jnp.where(i < nt - 1, hn_ref[0], jnp.zeros_like(hn_ref[0]))

    def up(fc, slot):
        for part in range(2):
            u_ref[slot, part] = _dot(hx_ref[...], wup_ref[part, fc])

    def conv_act(fc, slot):
        R = FFN_ROWS
        for rb in range(tm // R):
            r0 = H + rb * R
            for lt in range(FF_CHUNK // 128):
                ls = slice(lt * 128, (lt + 1) * 128)
                taps = []
                for part in range(2):
                    u = u_ref[slot, part, r0 - 8:r0 + R + 8, ls]
                    w = cwb_ref[part, fc, :, ls]
                    prev = pltpu.roll(u, 1, 0)[8:R + 8]
                    nxt = pltpu.roll(u, R + 15, 0)[8:R + 8]
                    taps.append(prev * w[0:1] + u[8:R + 8] * w[1:2] + nxt * w[2:3] + w[3:4])
                a, g = taps
                act_ref[fc, rb * R:(rb + 1) * R, ls] = ((g * _sigmoid(g)) * a).astype(BF16)

    up(0, 0)
    for s in range(1, N + 1):
        if s < N:
            up(s, s % 2)
        conv_act(s - 1, (s - 1) % 2)
    act = jnp.concatenate([act_ref[c] for c in range(N)], axis=1)
    o_ref[0] = x_ref[0] + mod_ref[0, 5:6, :] * _dot(act, wdn_ref[...])


def _ffn(h2, x1, mod, mod_row, wup, cwb, wdn, *, tm):
    B, T, D = x1.shape
    nt = T // tm
    H = FFN_HALO
    nh = T // H

    def tok(width):
        return pl.BlockSpec((1, tm, width), lambda i, b: (b, i, 0))

    def const(arr):
        nd = arr.ndim
        return pl.BlockSpec(arr.shape, lambda i, b: (0,) * nd, pipeline_mode=pl.Buffered(1))

    prev = pl.BlockSpec((1, H, D), lambda i, b: (b, jnp.maximum(i * (tm // H) - 1, 0), 0))
    nxt = pl.BlockSpec((1, H, D), lambda i, b: (b, jnp.minimum((i + 1) * (tm // H), nh - 1), 0))
    return pl.pallas_call(
        functools.partial(_ffn_kernel, tm=tm),
        grid=(nt, B),
        in_specs=[tok(D), prev, nxt, tok(D),
                  pl.BlockSpec((1, 6, D), lambda i, b: (mod_row(b), 0, 0)),
                  const(wup), const(cwb), const(wdn)],
        out_specs=tok(D),
        out_shape=jax.ShapeDtypeStruct((B, T, D), F32),
        scratch_shapes=[pltpu.VMEM((tm + 2 * H, D), BF16),
                        pltpu.VMEM((2, 2, tm + 2 * H, FF_CHUNK), F32),
                        pltpu.VMEM((N_FF_CHUNKS, tm, FF_CHUNK), BF16)],
        compiler_params=_params(2),
        name="ffn",
    )(h2, h2, h2, x1, mod, wup, cwb, wdn)


def _pad_heads(a, n_heads, width, padded):
    a = a.reshape(a.shape[:-1] + (n_heads, width))
    a = jnp.pad(a, [(0, 0)] * (a.ndim - 1) + [(0, padded - width)])
    return a.reshape(a.shape[:-2] + (n_heads * padded,))


def _permute_heads(a, axis):
    shape = a.shape
    a = a.reshape(shape[:axis] + (SWA_HEADS, HEAD_DIM) + shape[axis + 1:])
    a = jnp.take(a, jnp.array(SWA_HEAD_ORDER), axis=axis)
    return a.reshape(shape)


def _block_diag_ones(n, block):
    idx = np.arange(n) // block
    return jnp.asarray(idx[:, None] == idx[None, :], BF16)


def _pack_layer(w_in, gla_w_dec, gla_b_dec, q_norm_g, k_norm_g, pool_w, w_out, conv_w, conv_b):
    gq, gk, gv, gg, zf, zb, aq, ak, av, pu = jnp.split(
        w_in, np.cumsum([192, 192, 384, 384, 16, 16, 384, 128, 128])[:].tolist(), axis=-1)
    z = jnp.pad(jnp.concatenate([zf, zb], axis=-1), ((0, 0), (0, Z_PAD - 2 * GLA_RANK)))
    w = jnp.concatenate([_pad_heads(gq, GLA_HEADS, GLA_DK, GLA_DKP),
                         _pad_heads(gk, GLA_HEADS, GLA_DK, GLA_DKP),
                         gv, gg, z, av, _permute_heads(aq, 1), ak, pu], axis=-1).astype(BF16)
    wdec = jnp.zeros((Z_PAD, 2 * GLA_QKP), F32)
    wdec = wdec.at[0:GLA_RANK, :GLA_QKP].set(_pad_heads(gla_w_dec[0], GLA_HEADS, GLA_DK, GLA_DKP))
    wdec = wdec.at[GLA_RANK:2 * GLA_RANK, GLA_QKP:].set(_pad_heads(gla_w_dec[1], GLA_HEADS, GLA_DK, GLA_DKP))
    bdec = jnp.concatenate([_pad_heads(gla_b_dec[0], GLA_HEADS, GLA_DK, GLA_DKP),
                            _pad_heads(gla_b_dec[1], GLA_HEADS, GLA_DK, GLA_DKP)])[None, :]
    qkg = jnp.concatenate([jnp.tile(q_norm_g, SWA_HEADS) * (HEAD_DIM ** -0.5),
                           jnp.tile(k_norm_g, SWA_KV_HEADS)])[None, :]
    poolw = jax.scipy.linalg.block_diag(*[pool_w[g] for g in range(len(POOL_WINDOWS))]).astype(BF16)
    wout = jnp.concatenate([w_out[:GLA_V], _permute_heads(w_out[GLA_V:GLA_V + SWA_Q], 0),
                            w_out[GLA_V + SWA_Q:]], axis=0).astype(BF16)
    cwb = jnp.concatenate([conv_w, conv_b[None, :]], axis=0).reshape(4, 2, N_FF_CHUNKS, FF_CHUNK)
    return dict(w=w, wdec=wdec.astype(BF16), bdec=bdec, qkg=qkg, poolw=poolw, wout=wout,
                cwb=jnp.transpose(cwb, (1, 2, 0, 3)))


def _rope_tables(n_tokens):
    rows_n = n_tokens // GRID_W
    rows = jnp.repeat(jnp.arange(rows_n), GRID_W).astype(F32)
    cols = jnp.tile(jnp.arange(GRID_W), rows_n).astype(F32)
    nf = HEAD_DIM // 4
    inv = ROPE_BASE ** (-jnp.arange(nf, dtype=F32) / nf)
    ar = rows[:, None] * inv
    ac = cols[:, None] * inv
    cos = jnp.concatenate([jnp.cos(ar), jnp.cos(ar), jnp.cos(ac), jnp.cos(ac)], axis=-1)
    sin = jnp.concatenate([-jnp.sin(ar), jnp.sin(ar), -jnp.sin(ac), jnp.sin(ac)], axis=-1)
    return jnp.tile(cos, (1, 128 // HEAD_DIM)), jnp.tile(sin, (1, 128 // HEAD_DIM))


def _pool_inv_counts(n_tokens):
    t = jnp.arange(n_tokens)[:, None]
    half = jnp.repeat(jnp.asarray(POOL_WINDOWS) // 2, POOL_GROUP)[None, :]
    cnt = jnp.minimum(t + half, n_tokens) - jnp.maximum(t - half, 0)
    return 1.0 / cnt.astype(F32)


def _tile_sizes(T):
    return dict(tm=min(T, 512), tg=min(T, 512))


def kernel(x, c, ctx, c_ctx, w_ada, b_ada, norm1_g, w_in, gla_w_dec, gla_b_dec, gla_norm_g, q_norm_g,
           k_norm_g, sink_logit, pool_w, pool_scale, w_out, norm2_g, w_up, conv_w, conv_b, w_down):
    B, S, D = x.shape
    L = ctx.shape[1]
    c_rows = jnp.zeros((8, D), F32).at[:B].set(c).at[B].set(c_ctx)
    mod_all = _modulation(c_rows, w_ada, b_ada).reshape(DEPTH, 8, 6, D)
    cos, sin = _rope_tables(S)
    tri = jnp.stack([jnp.tril(jnp.ones((GLA_CHUNK, GLA_CHUNK), BF16)),
                     jnp.triu(jnp.ones((GLA_CHUNK, GLA_CHUNK), BF16))])
    ones64 = _block_diag_ones(SWA_Q + SWA_KV, HEAD_DIM) * jnp.asarray(1.0 / HEAD_DIM, BF16)
    ones96 = _block_diag_ones(GLA_V, GLA_DV)
    icnt_lat, icnt_ctx = _pool_inv_counts(S), _pool_inv_counts(L)
    bd = jnp.asarray((np.arange(GLA_V) // GLA_DV)[:, None] == (np.arange(GLA_QKP) // GLA_DKP)[None, :], F32)
    lat, cx = _tile_sizes(S), _tile_sizes(L)
    lat_row = lambda b: b
    ctx_row = lambda b: B
    zero_state = jnp.zeros((B, 2, GLA_V, GLA_QKP), F32)

    for l in range(DEPTH):
        p = _pack_layer(w_in[l], gla_w_dec[l], gla_b_dec[l], q_norm_g[l], k_norm_g[l], pool_w[l],
                        w_out[l], conv_w[l], conv_b[l])
        mod = mod_all[l]
        g1 = norm1_g[l][None, :]
        g2 = norm2_g[l][None, :]
        gng = jnp.tile(gla_norm_g[l], GLA_HEADS)[None, :]
        pools = pool_scale[l][None, :]
        wup = jnp.transpose(w_up[l].astype(BF16).reshape(D, 2, N_FF_CHUNKS, FF_CHUNK), (1, 2, 0, 3))
        wdn = w_down[l].astype(BF16)
        shared = (g1, p["w"], p["wdec"], p["bdec"], tri, ones64, p["qkg"])
        update_ctx = l < DEPTH - 1

        (cqdf, ckif, ckef, cqdb, ckib, ckeb, cdecf, cdecb, cgv, cgg, csq, cskv, cpu) = _in_proj(
            ctx, mod, ctx_row, *shared, cos[:L], sin[:L], tm=cx["tm"], rope=False)
        co_f, co_b, st = _gla(cqdf, ckif, ckef, cqdb, ckib, ckeb, cdecf, cdecb, cgv, zero_state, bd,
                              tg=cx["tg"])

        (qdf, kif, kef, qdb, kib, keb, decf, decb, gv, gg, sq, skv, pu) = _in_proj(
            x, mod, lat_row, *shared, cos, sin, tm=lat["tm"], rope=True)
        o_f, o_b, _ = _gla(qdf, kif, kef, qdb, kib, keb, decf, decb, gv, st, bd, tg=lat["tg"])
        swa = _swa(sq, skv, cskv, sink_logit[l], local=True)
        x1, h2 = _out_proj(x, o_f, o_b, gg, swa, pu, icnt_lat, mod, lat_row, gng, ones96, p["poolw"],
                           pools, p["wout"], g2, tm=lat["tm"])
        x = _ffn(h2, x1, mod, lat_row, wup, p["cwb"], wdn, tm=lat["tm"])

        if update_ctx:
            cswa = _swa(csq, None, cskv, sink_logit[l], local=False)
            c1, ch2 = _out_proj(ctx, co_f, co_b, cgg, cswa, cpu, icnt_ctx, mod, ctx_row, gng, ones96,
                                p["poolw"], pools, p["wout"], g2, tm=cx["tm"])
            ctx = _ffn(ch2, c1, mod, ctx_row, wup, p["cwb"], wdn, tm=cx["tm"])
    return x
```

```python
import functools

import jax
import jax.numpy as jnp
import numpy as np
from jax import lax
from jax.experimental import pallas as pl
from jax.experimental.pallas import tpu as pltpu

F32 = jnp.float32
BF16 = jnp.bfloat16

D_MODEL = 1024
DEPTH = 2
GRID_W = 64
GLA_HEADS = 4
GLA_DK = 48
GLA_DKP = 64
GLA_DV = 96
GLA_RANK = 16
GLA_TAU = 16.0
GLA_CHUNK = 64
SWA_HEADS = 6
SWA_KV_HEADS = 2
SWA_GROUP = SWA_HEADS // SWA_KV_HEADS
HEAD_DIM = 64
WINDOW = 128
Q_BLOCK = 128
ROPE_BASE = 10000.0
POOL_WINDOWS = (2, 4, 8, 16)
POOL_GROUP = 64
POOL_HALF_MAX = max(POOL_WINDOWS) // 2
POOL_ROWS = 128
D_FF = 2816
EPS = 1e-6
NEG_INF = -1e30
LOG2E = 1.4426950408889634

GLA_QKP = GLA_HEADS * GLA_DKP
GLA_V = GLA_HEADS * GLA_DV
SWA_Q = SWA_HEADS * HEAD_DIM
SWA_KV = SWA_KV_HEADS * HEAD_DIM
POOL_W = len(POOL_WINDOWS) * POOL_GROUP
Z_PAD = 128

OFF_Q = 0
OFF_K = OFF_Q + GLA_QKP
OFF_GV = OFF_K + GLA_QKP
OFF_GG = OFF_GV + GLA_V
OFF_Z = OFF_GG + GLA_V
OFF_SV = OFF_Z + Z_PAD
OFF_SQ = OFF_SV + SWA_KV
OFF_SK = OFF_SQ + SWA_Q
OFF_PU = OFF_SK + SWA_KV
IN_WP = OFF_PU + POOL_W

SWA_HEAD_ORDER = tuple(h for t in range(SWA_GROUP) for h in (t, t + SWA_GROUP))

FF_CHUNK = 256
N_FF_CHUNKS = D_FF // FF_CHUNK
FFN_HALO = 16
TOKEN_RESIDUES = 8
V7X_VMEM_LIMIT = 56 * 1024 * 1024

_NT = (((1,), (1,)), ((), ()))
_TN = (((0,), (0,)), ((), ()))


def _dot(a, b):
    return jnp.dot(a, b, preferred_element_type=F32)


def _params(n_grid, flags=None):
    return pltpu.CompilerParams(dimension_semantics=("arbitrary",) * n_grid,
                                vmem_limit_bytes=V7X_VMEM_LIMIT, flags=flags)


def _sigmoid(x):
    return 1.0 / (1.0 + jnp.exp(-x))


def _log_sigmoid(x):
    return jnp.minimum(x, 0.0) - jnp.log1p(jnp.exp(-jnp.abs(x)))


def _rms(x):
    return x * lax.rsqrt(jnp.mean(x * x, axis=-1, keepdims=True) + EPS)


def _mod_kernel(c_ref, w_ref, b_ref, o_ref):
    c = c_ref[...]
    act = (c * _sigmoid(c)).astype(BF16)
    o_ref[0] = _dot(act, w_ref[0].astype(BF16)) + b_ref[0]


def _modulation(c_rows, w_ada, b_ada):
    tn = 512
    n = w_ada.shape[-1]
    return pl.pallas_call(
        _mod_kernel,
        grid=(DEPTH, n // tn),
        in_specs=[pl.BlockSpec((8, D_MODEL), lambda l, j: (0, 0)),
                  pl.BlockSpec((1, D_MODEL, tn), lambda l, j: (l, 0, j)),
                  pl.BlockSpec((1, 1, tn), lambda l, j: (l, 0, j))],
        out_specs=pl.BlockSpec((1, 8, tn), lambda l, j: (l, 0, j)),
        out_shape=jax.ShapeDtypeStruct((DEPTH, 8, n), F32),
        compiler_params=_params(2),
        name="adaln_mod",
    )(c_rows, w_ada, b_ada.reshape(DEPTH, 1, n))


def _in_proj_kernel(x_ref, mod_ref, g1_ref, w_ref, wdec_ref, bdec_ref, tri_ref, ones_ref, qkg_ref,
                    cos_ref, sin_ref,
                    qdf_ref, kif_ref, kef_ref, qdb_ref, kib_ref, keb_ref, decf_ref, decb_ref,
                    gv_ref, gg_ref, sq_ref, skv_ref, pu_ref, *, tm, rope):
    x = x_ref[0]
    sh1 = mod_ref[0, 0:1, :]
    sc1 = mod_ref[0, 1:2, :]
    h = _rms(x) * (g1_ref[...] * (1.0 + sc1)) + sh1
    hb = h.astype(BF16)

    qk = _dot(hb, w_ref[:, OFF_Q:OFF_GV])
    q = qk[:, :GLA_QKP] * (GLA_DK ** -0.5)
    k = qk[:, GLA_QKP:]
    zsv = _dot(hb, w_ref[:, OFF_Z:OFF_SQ])
    z = zsv[:, :Z_PAD]
    skv_ref[0, :, SWA_KV:] = zsv[:, Z_PAD:].astype(BF16)
    la = _log_sigmoid(_dot(z.astype(BF16), wdec_ref[...]) + bdec_ref[...]) * (LOG2E / GLA_TAU)
    la_hi = la.astype(BF16)
    la_lo = (la - la_hi.astype(F32)).astype(BF16)
    lower = tri_ref[0]
    upper = tri_ref[1]
    C = GLA_CHUNK
    for c in range(tm // C):
        r = slice(c * C, (c + 1) * C)
        bcf = _dot(lower, la_hi[r, :GLA_QKP]) + _dot(lower, la_lo[r, :GLA_QKP])
        bcb = _dot(upper, la_hi[r, GLA_QKP:]) + _dot(upper, la_lo[r, GLA_QKP:])
        blf = bcf[C - 1:C, :]
        blb = bcb[0:1, :]
        qc = q[r]
        kc = k[r]
        qdf_ref[0, r, :] = (qc * jnp.exp2(bcf)).astype(BF16)
        kif_ref[0, r, :] = (kc * jnp.exp2(-bcf)).astype(BF16)
        kef_ref[0, r, :] = (kc * jnp.exp2(blf - bcf)).astype(BF16)
        qdb_ref[0, r, :] = (qc * jnp.exp2(bcb)).astype(BF16)
        kib_ref[0, r, :] = (kc * jnp.exp2(-bcb)).astype(BF16)
        keb_ref[0, r, :] = (kc * jnp.exp2(blb - bcb)).astype(BF16)
        decf_ref[0, c] = jnp.exp2(blf)
        decb_ref[0, c] = jnp.exp2(blb)

    vg = _dot(hb, w_ref[:, OFF_GV:OFF_Z])
    gv_ref[0] = vg[:, :GLA_V].astype(BF16)
    gg_ref[0] = vg[:, GLA_V:].astype(BF16)

    sqk = _dot(hb, w_ref[:, OFF_SQ:OFF_PU])
    ss = _dot((sqk * sqk).astype(BF16), ones_ref[...])
    sqk = sqk * lax.rsqrt(ss + EPS) * qkg_ref[...]
    tiles = []
    for t in range((SWA_Q + SWA_KV) // 128):
        xt = sqk[:, t * 128:(t + 1) * 128]
        if rope:
            lane = lax.broadcasted_iota(jnp.int32, xt.shape, 1)
            first = (lane % 32) < 16
            partner = jnp.where(first, pltpu.roll(xt, 128 - 16, 1), pltpu.roll(xt, 16, 1))
            xt = xt * cos_ref[...] + partner * sin_ref[...]
        tiles.append(xt.astype(BF16))
    for t in range(SWA_Q // 128):
        sq_ref[0, :, t * 128:(t + 1) * 128] = tiles[t]
    skv_ref[0, :, :SWA_KV] = tiles[-1]

    pu_ref[0] = _dot(hb, w_ref[:, OFF_PU:IN_WP])


def _in_proj(x, mod, mod_row, g1, w, wdec, bdec, tri, ones64, qkg, cos, sin, *, tm, rope):
    B, T, D = x.shape
    nt = T // tm
    nc = T // GLA_CHUNK

    def tok(width, dtype):
        return (jax.ShapeDtypeStruct((B, T, width), dtype),
                pl.BlockSpec((1, tm, width), lambda i, b: (b, i, 0)))

    def const(arr):
        nd = arr.ndim
        return pl.BlockSpec(arr.shape, lambda i, b: (0,) * nd)

    dec = (jax.ShapeDtypeStruct((B, nc, 1, GLA_QKP), F32),
           pl.BlockSpec((1, tm // GLA_CHUNK, 1, GLA_QKP), lambda i, b: (b, i, 0, 0)))
    outs = [tok(GLA_QKP, BF16)] * 6 + [dec, dec] + [tok(GLA_V, BF16), tok(GLA_V, BF16),
                                                   tok(SWA_Q, BF16), tok(2 * SWA_KV, BF16),
                                                   tok(POOL_W, F32)]
    return pl.pallas_call(
        functools.partial(_in_proj_kernel, tm=tm, rope=rope),
        grid=(nt, B),
        in_specs=[pl.BlockSpec((1, tm, D), lambda i, b: (b, i, 0)),
                  pl.BlockSpec((1, 6, D), lambda i, b: (mod_row(b), 0, 0)),
                  const(g1), const(w), const(wdec), const(bdec), const(tri), const(ones64), const(qkg),
                  pl.BlockSpec((tm, 128), lambda i, b: (i, 0)),
                  pl.BlockSpec((tm, 128), lambda i, b: (i, 0))],
        out_specs=[o[1] for o in outs],
        out_shape=[o[0] for o in outs],
        compiler_params=_params(2),
        name="in_proj",
    )(x, mod, g1, w, wdec, bdec, tri, ones64, qkg, cos, sin)


def _gla_kernel(qdf_ref, kif_ref, kef_ref, vf_ref, decf_ref, qdb_ref, kib_ref, keb_ref, vb_ref, decb_ref,
                s0_ref, bd_ref, of_ref, ob_ref, sfin_ref, st_ref, *, tg):
    i = pl.program_id(1)
    nb = pl.num_programs(1)
    C = GLA_CHUNK
    ncb = tg // C

    @pl.when(i == 0)
    def _():
        st_ref[...] = s0_ref[0]

    HC = GLA_HEADS * C
    row_head = lax.broadcasted_iota(jnp.int32, (HC, GLA_QKP), 0) // C
    k_mask = row_head == lax.broadcasted_iota(jnp.int32, (HC, GLA_QKP), 1) // GLA_DKP
    vlane = lax.broadcasted_iota(jnp.int32, (HC, GLA_V), 1)
    vhead = sum((vlane >= h * GLA_DV).astype(jnp.int32) for h in range(1, GLA_HEADS))
    v_mask = lax.broadcasted_iota(jnp.int32, (HC, GLA_V), 0) // C == vhead
    row_i = lax.broadcasted_iota(jnp.int32, (C, HC), 0)
    col_j = lax.broadcasted_iota(jnp.int32, (C, HC), 1) % C
    bd = bd_ref[...]

    def chunk(d, qd_ref, ki_ref, ke_ref, v_ref, dec_ref, o_ref, c):
        r = slice(c * C, (c + 1) * C)
        qd = qd_ref[0, r, :]
        ki = ki_ref[0, r, :]
        ke = ke_ref[0, r, :]
        vv = v_ref[0, r, :]
        k_bd = jnp.where(k_mask, jnp.concatenate([ki] * GLA_HEADS, axis=0), jnp.zeros((), BF16))
        v_bd = jnp.where(v_mask, jnp.concatenate([vv] * GLA_HEADS, axis=0), jnp.zeros((), BF16))
        st = st_ref[d]
        rhs = jnp.concatenate([k_bd, st.astype(BF16)], axis=0)
        res = lax.dot_general(qd, rhs, _NT, preferred_element_type=F32)
        keep = (col_j <= row_i) if d == 0 else (col_j >= row_i)
        p = jnp.where(keep, res[:, :HC], 0.0).astype(BF16)
        o = res[:, HC:] + _dot(p, v_bd)
        o_ref[0, r, :] = o.astype(BF16)
        upd = lax.dot_general(vv, ke, _TN, preferred_element_type=F32)
        st_ref[d] = st * dec_ref[0, c] + upd * bd

    for c in range(ncb):
        chunk(0, qdf_ref, kif_ref, kef_ref, vf_ref, decf_ref, of_ref, c)
        chunk(1, qdb_ref, kib_ref, keb_ref, vb_ref, decb_ref, ob_ref, ncb - 1 - c)

    @pl.when(i == nb - 1)
    def _():
        sfin_ref[0] = st_ref[...]


def _gla(qdf, kif, kef, qdb, kib, keb, decf, decb, v, s0, bd, *, tg):
    B, T, _ = v.shape
    nb = T // tg
    ncb = tg // GLA_CHUNK

    def fwd(width):
        return pl.BlockSpec((1, tg, width), lambda b, i: (b, i, 0))

    def bwd(width):
        return pl.BlockSpec((1, tg, width), lambda b, i: (b, nb - 1 - i, 0))

    dec_f = pl.BlockSpec((1, ncb, 1, GLA_QKP), lambda b, i: (b, i, 0, 0))
    dec_b = pl.BlockSpec((1, ncb, 1, GLA_QKP), lambda b, i: (b, nb - 1 - i, 0, 0))
    st_spec = pl.BlockSpec((1, 2, GLA_V, GLA_QKP), lambda b, i: (b, 0, 0, 0))
    return pl.pallas_call(
        functools.partial(_gla_kernel, tg=tg),
        grid=(B, nb),
        in_specs=[fwd(GLA_QKP), fwd(GLA_QKP), fwd(GLA_QKP), fwd(GLA_V), dec_f,
                  bwd(GLA_QKP), bwd(GLA_QKP), bwd(GLA_QKP), bwd(GLA_V), dec_b,
                  st_spec, pl.BlockSpec(bd.shape, lambda b, i: (0, 0))],
        out_specs=[fwd(GLA_V), bwd(GLA_V), st_spec],
        out_shape=[jax.ShapeDtypeStruct((B, T, GLA_V), BF16),
                   jax.ShapeDtypeStruct((B, T, GLA_V), BF16),
                   jax.ShapeDtypeStruct((B, 2, GLA_V, GLA_QKP), F32)],
        scratch_shapes=[pltpu.VMEM((2, GLA_V, GLA_QKP), F32)],
        compiler_params=_params(2),
        name="gla",
    )(qdf, kif, kef, v, decf, qdb, kib, keb, v, decb, s0, bd)


def _swa_kernel(*refs, n_local, seq_len):
    if n_local:
        q_ref, kvp_ref, kvc_ref, kvn_ref, ckv_ref, sink_ref, o_ref = refs
        kv_parts = [kvp_ref, kvc_ref, kvn_ref, ckv_ref]
    else:
        q_ref, ckv_ref, sink_ref, o_ref = refs
        kv_parts = [ckv_ref]
    n = pl.program_id(1)
    QB = Q_BLOCK
    kcat = jnp.concatenate([r[0, :, :SWA_KV] for r in kv_parts], axis=0)
    vcat = jnp.concatenate([r[0, :, SWA_KV:] for r in kv_parts], axis=0)

    lane = lax.broadcasted_iota(jnp.int32, (QB, 128), 1)
    low = lane < HEAD_DIM
    n_loc = n_local * QB
    if n_local:
        qi = lax.broadcasted_iota(jnp.int32, (QB, n_loc), 0)
        kj = lax.broadcasted_iota(jnp.int32, (QB, n_loc), 1)
        k_pos = (n - 1) * QB + kj
        dist = kj - QB - qi
        valid = (dist >= -WINDOW) & (dist <= WINDOW) & (k_pos >= 0) & (k_pos < seq_len)

    def scores(g):
        slabs = []
        for t in range(SWA_GROUP):
            qt = q_ref[0, :, t * 128:(t + 1) * 128]
            slabs.append(jnp.where(low if g == 0 else ~low, qt, jnp.zeros_like(qt)))
        return lax.dot_general(jnp.concatenate(slabs, axis=0), kcat, _NT,
                               preferred_element_type=F32)

    def attend(g, s_grp):
        ps, dens = [], []
        for t in range(SWA_GROUP):
            sink = sink_ref[g * SWA_GROUP + t] * LOG2E
            sh = s_grp[t * QB:(t + 1) * QB]
            s_ctx = sh[:, n_loc:]
            m = jnp.maximum(jnp.max(s_ctx, axis=-1, keepdims=True), sink)
            if n_local:
                s_loc = jnp.where(valid, sh[:, :n_loc], NEG_INF)
                m = jnp.maximum(m, jnp.max(s_loc, axis=-1, keepdims=True))
                p_loc = jnp.exp2(s_loc - m)
            p_ctx = jnp.exp2(s_ctx - m)
            den = jnp.sum(p_ctx, axis=-1, keepdims=True) + jnp.exp2(sink - m)
            if n_local:
                den = den + jnp.sum(p_loc, axis=-1, keepdims=True)
                ps.append(jnp.concatenate([p_loc, p_ctx], axis=1).astype(BF16))
            else:
                ps.append(p_ctx.astype(BF16))
            dens.append(den)
        o_grp = _dot(jnp.concatenate(ps, axis=0), vcat)
        return [o_grp[t * QB:(t + 1) * QB] / dens[t] for t in range(SWA_GROUP)]

    s_groups = [scores(g) for g in range(SWA_KV_HEADS)]
    o_a = attend(0, s_groups[0])
    o_b = attend(1, s_groups[1])
    for t in range(SWA_GROUP):
        o_ref[0, :, t * 128:(t + 1) * 128] = jnp.where(low, o_a[t], o_b[t]).astype(BF16)


def _swa(q, kv, ckv, sink, *, local):
    B, S, _ = q.shape
    L = ckv.shape[1]
    nq = S // Q_BLOCK
    qspec = pl.BlockSpec((1, Q_BLOCK, SWA_Q), lambda b, n: (b, n, 0))
    cspec = pl.BlockSpec((1, L, 2 * SWA_KV), lambda b, n: (b, 0, 0))
    sspec = pl.BlockSpec(memory_space=pltpu.SMEM)
    if local:
        def blk(off):
            return pl.BlockSpec((1, Q_BLOCK, 2 * SWA_KV),
                                lambda b, n: (b, jnp.clip(n + off, 0, nq - 1), 0))
        in_specs = [qspec, blk(-1), blk(0), blk(1), cspec, sspec]
        args = (q, kv, kv, kv, ckv, sink)
    else:
        in_specs = [qspec, cspec, sspec]
        args = (q, ckv, sink)
    return pl.pallas_call(
        functools.partial(_swa_kernel, n_local=3 if local else 0, seq_len=S),
        grid=(B, nq),
        in_specs=in_specs,
        out_specs=qspec,
        out_shape=jax.ShapeDtypeStruct((B, S, SWA_Q), BF16),
        compiler_params=_params(2),
        name="swa",
    )(*args)


def _out_kernel(x_ref, of_ref, ob_ref, gg_ref, swa_ref, pu_ref, pup_ref, pun_ref, icnt_ref, mod_ref,
                gng_ref, ones_ref, poolw_ref, pools_ref, wout_ref, g2_ref,
                x1_ref, h2_ref, ext_ref, cen_ref, hs_ref, *, tm):
    i = pl.program_id(0)
    nt = pl.num_programs(0)
    HP = POOL_HALF_MAX

    o = of_ref[0].astype(F32) + ob_ref[0].astype(F32)
    ss = _dot((o * o).astype(BF16), ones_ref[...])
    gg = gg_ref[0].astype(F32)
    gla = o * lax.rsqrt(ss * (1.0 / GLA_DV) + EPS) * gng_ref[...] * (gg * _sigmoid(gg))

    ext_ref[0:HP, :] = jnp.where(i > 0, pup_ref[0], 0.0)
    ext_ref[HP:HP + tm, :] = pu_ref[0]
    ext_ref[HP + tm:, :] = jnp.where(i < nt - 1, pun_ref[0], 0.0)
    upper_group = lax.broadcasted_iota(jnp.int32, (POOL_ROWS, 128), 1) >= POOL_GROUP
    for rb in range(tm // POOL_ROWS):
        r0 = HP + rb * POOL_ROWS
        rows = slice(rb * POOL_ROWS, (rb + 1) * POOL_ROWS)
        for tile in range(POOL_W // 128):
            ls = slice(tile * 128, (tile + 1) * 128)
            acc = None
            win = None
            done = 0
            for w in POOL_WINDOWS[2 * tile:2 * tile + 2]:
                hw = w // 2
                for d in list(range(-hw, -done)) + list(range(done, hw)):
                    term = ext_ref[r0 + d:r0 + d + POOL_ROWS, ls]
                    acc = term if acc is None else acc + term
                done = hw
                win = acc if win is None else jnp.where(upper_group, acc, win)
            centred = win * icnt_ref[rows, ls] - ext_ref[r0:r0 + POOL_ROWS, ls]
            cen_ref[rows, ls] = centred.astype(BF16)
    pool = _dot(cen_ref[...], poolw_ref[...]) * pools_ref[...]

    cat = jnp.concatenate([gla.astype(BF16), swa_ref[0], pool.astype(BF16)], axis=1)
    y = _dot(cat, wout_ref[...])
    g1 = mod_ref[0, 2:3, :]
    sh2 = mod_ref[0, 3:4, :]
    sc2 = mod_ref[0, 4:5, :]
    x1 = x_ref[0] + g1 * y
    x1_ref[0] = x1
    h2 = _rms(x1) * (g2_ref[...] * (1.0 + sc2)) + sh2
    J = tm // TOKEN_RESIDUES
    for lt in range(D_MODEL // 128):
        ls = slice(lt * 128, (lt + 1) * 128)
        hs_ref[lt] = h2[:, ls]
        for k in range(TOKEN_RESIDUES):
            h2_ref[0, k * J:(k + 1) * J, ls] = hs_ref[lt, pl.ds(k, J, stride=TOKEN_RESIDUES), :].astype(BF16)


def _out_proj(x, o_f, o_b, gg, swa, pu, icnt, mod, mod_row, gng, ones96, poolw, pools, wout, g2, *, tm):
    B, T, D = x.shape
    nt = T // tm
    HP = POOL_HALF_MAX
    nh = T // HP

    def tok(width):
        return pl.BlockSpec((1, tm, width), lambda i, b: (b, i, 0))

    def const(arr):
        nd = arr.ndim
        return pl.BlockSpec(arr.shape, lambda i, b: (0,) * nd)

    prev = pl.BlockSpec((1, HP, POOL_W), lambda i, b: (b, jnp.maximum(i * (tm // HP) - 1, 0), 0))
    nxt = pl.BlockSpec((1, HP, POOL_W), lambda i, b: (b, jnp.minimum((i + 1) * (tm // HP), nh - 1), 0))
    return pl.pallas_call(
        functools.partial(_out_kernel, tm=tm),
        grid=(nt, B),
        in_specs=[tok(D), tok(GLA_V), tok(GLA_V), tok(GLA_V), tok(SWA_Q), tok(POOL_W), prev, nxt,
                  pl.BlockSpec((tm, POOL_W), lambda i, b: (i, 0)),
                  pl.BlockSpec((1, 6, D), lambda i, b: (mod_row(b), 0, 0)),
                  const(gng), const(ones96), const(poolw), const(pools), const(wout), const(g2)],
        out_specs=[tok(D), tok(D)],
        out_shape=[jax.ShapeDtypeStruct((B, T, D), F32), jax.ShapeDtypeStruct((B, T, D), BF16)],
        scratch_shapes=[pltpu.VMEM((tm + 2 * HP, POOL_W), F32), pltpu.VMEM((tm, POOL_W), BF16),
                        pltpu.VMEM((D // 128, tm, 128), F32)],
        compiler_params=_params(2),
        name="out_proj",
    )(x, o_f, o_b, gg, swa, pu, pu, pu, icnt, mod, gng, ones96, poolw, pools, wout, g2)


def _ffn_kernel(h_ref, hp_ref, hn_ref, x_ref, mod_ref, wup_ref, cwb_ref, wdn_ref,
                o_ref, hx_ref, u_ref, act_ref, nat_ref, *, tm):
    i = pl.program_id(0)
    nt = pl.num_programs(0)
    H = FFN_HALO
    N = N_FF_CHUNKS
    K = TOKEN_RESIDUES
    J = tm // K
    hx_ref[0:tm, :] = h_ref[0]
    hx_ref[tm:tm + H, :] = jnp.where(i > 0, hp_ref[0], jnp.zeros_like(hp_ref[0]))
    hx_ref[tm + H:, :] = jnp.where(i < nt - 1, hn_ref[0], jnp.zeros_like(hn_ref[0]))
    row8 = lax.broadcasted_iota(jnp.int32, (8, 128), 0)

    def up(fc, slot):
        for part in range(2):
            c0 = part * D_FF + fc * FF_CHUNK
            u_ref[slot, part] = _dot(hx_ref[...], wup_ref[:, c0:c0 + FF_CHUNK])

    def conv_act(fc, slot):
        for lt in range(FF_CHUNK // 128):
            ls = slice(lt * 128, (lt + 1) * 128)
            wts = [cwb_ref[part, fc, :, ls] for part in range(2)]
            for k in range(K):
                taps = []
                for part in range(2):
                    u = u_ref.at[slot, part]
                    if k > 0:
                        prev = u[(k - 1) * J:k * J, ls]
                    else:
                        rolled = pltpu.roll(u[(K - 1) * J:K * J, ls], 1, 0)
                        head = jnp.where(row8 == 0, u[tm + H - 1:tm + H, ls], rolled[0:8])
                        prev = jnp.concatenate([head, rolled[8:]], axis=0)
                    if k < K - 1:
                        nxt = u[(k + 1) * J:(k + 2) * J, ls]
                    else:
                        rolled = pltpu.roll(u[0:J, ls], J - 1, 0)
                        tail = jnp.where(row8 == 7, u[tm + H:tm + H + 1, ls], rolled[J - 8:J])
                        nxt = jnp.concatenate([rolled[:J - 8], tail], axis=0)
                    w = wts[part]
                    taps.append(prev * w[0:1] + u[k * J:(k + 1) * J, ls] * w[1:2] + nxt * w[2:3] + w[3:4])
                a, g = taps
                act_ref[fc, k * J:(k + 1) * J, ls] = ((g * _sigmoid(g)) * a).astype(BF16)

    up(0, 0)
    for s in range(1, N + 1):
        if s < N:
            up(s, s % 2)
        conv_act(s - 1, (s - 1) % 2)
    act = jnp.concatenate([act_ref[c] for c in range(N)], axis=1)
    y = mod_ref[0, 5:6, :] * _dot(act, wdn_ref[...])
    for lt in range(D_MODEL // 128):
        ls = slice(lt * 128, (lt + 1) * 128)
        for k in range(K):
            nat_ref[lt, pl.ds(k, J, stride=K), :] = y[k * J:(k + 1) * J, ls]
        o_ref[0, :, ls] = x_ref[0, :, ls] + nat_ref[lt]


def _ffn(h2, x1, mod, mod_row, wup, cwb, wdn, *, tm):
    B, T, D = x1.shape
    nt = T // tm
    H = FFN_HALO
    nh = T // H

    def tok(width):
        return pl.BlockSpec((1, tm, width), lambda i, b: (b, i, 0))

    def const(arr):
        nd = arr.ndim
        return pl.BlockSpec(arr.shape, lambda i, b: (0,) * nd)

    prev = pl.BlockSpec((1, H, D), lambda i, b: (b, jnp.maximum(i * (tm // H) - 1, 0), 0))
    nxt = pl.BlockSpec((1, H, D), lambda i, b: (b, jnp.minimum((i + 1) * (tm // H), nh - 1), 0))
    return pl.pallas_call(
        functools.partial(_ffn_kernel, tm=tm),
        grid=(nt, B),
        in_specs=[tok(D), prev, nxt, tok(D),
                  pl.BlockSpec((1, 6, D), lambda i, b: (mod_row(b), 0, 0)),
                  const(wup), const(cwb), const(wdn)],
        out_specs=tok(D),
        out_shape=jax.ShapeDtypeStruct((B, T, D), F32),
        scratch_shapes=[pltpu.VMEM((tm + 2 * H, D), BF16),
                        pltpu.VMEM((2, 2, tm + 2 * H, FF_CHUNK), F32),
                        pltpu.VMEM((N_FF_CHUNKS, tm, FF_CHUNK), BF16),
                        pltpu.VMEM((D // 128, tm, 128), F32)],
        compiler_params=_params(2),
        name="ffn",
    )(h2, h2, h2, x1, mod, wup, cwb, wdn)


def _pad_heads(a, n_heads, width, padded):
    a = a.reshape(a.shape[:-1] + (n_heads, width))
    a = jnp.pad(a, [(0, 0)] * (a.ndim - 1) + [(0, padded - width)])
    return a.reshape(a.shape[:-2] + (n_heads * padded,))


def _permute_heads(a, axis):
    shape = a.shape
    a = a.reshape(shape[:axis] + (SWA_HEADS, HEAD_DIM) + shape[axis + 1:])
    a = jnp.take(a, jnp.array(SWA_HEAD_ORDER), axis=axis)
    return a.reshape(shape)


def _block_diag_ones(n, block):
    idx = np.arange(n) // block
    return jnp.asarray(idx[:, None] == idx[None, :], BF16)


def _pack_layer(w_in, gla_w_dec, gla_b_dec, q_norm_g, k_norm_g, pool_w, w_out, conv_w, conv_b):
    gq, gk, gv, gg, zf, zb, aq, ak, av, pu = jnp.split(
        w_in, np.cumsum([192, 192, 384, 384, 16, 16, 384, 128, 128])[:].tolist(), axis=-1)
    z = jnp.pad(jnp.concatenate([zf, zb], axis=-1), ((0, 0), (0, Z_PAD - 2 * GLA_RANK)))
    w = jnp.concatenate([_pad_heads(gq, GLA_HEADS, GLA_DK, GLA_DKP),
                         _pad_heads(gk, GLA_HEADS, GLA_DK, GLA_DKP),
                         gv, gg, z, av, _permute_heads(aq, 1), ak, pu], axis=-1).astype(BF16)
    wdec = jnp.zeros((Z_PAD, 2 * GLA_QKP), F32)
    wdec = wdec.at[0:GLA_RANK, :GLA_QKP].set(_pad_heads(gla_w_dec[0], GLA_HEADS, GLA_DK, GLA_DKP))
    wdec = wdec.at[GLA_RANK:2 * GLA_RANK, GLA_QKP:].set(_pad_heads(gla_w_dec[1], GLA_HEADS, GLA_DK, GLA_DKP))
    bdec = jnp.concatenate([_pad_heads(gla_b_dec[0], GLA_HEADS, GLA_DK, GLA_DKP),
                            _pad_heads(gla_b_dec[1], GLA_HEADS, GLA_DK, GLA_DKP)])[None, :]
    qkg = jnp.concatenate([jnp.tile(q_norm_g, SWA_HEADS) * (HEAD_DIM ** -0.5 * LOG2E),
                           jnp.tile(k_norm_g, SWA_KV_HEADS)])[None, :]
    poolw = jax.scipy.linalg.block_diag(*[pool_w[g] for g in range(len(POOL_WINDOWS))]).astype(BF16)
    wout = jnp.concatenate([w_out[:GLA_V], _permute_heads(w_out[GLA_V:GLA_V + SWA_Q], 0),
                            w_out[GLA_V + SWA_Q:]], axis=0).astype(BF16)
    cwb = jnp.concatenate([conv_w, conv_b[None, :]], axis=0).reshape(4, 2, N_FF_CHUNKS, FF_CHUNK)
    return dict(w=w, wdec=wdec.astype(BF16), bdec=bdec, qkg=qkg, poolw=poolw, wout=wout,
                cwb=jnp.transpose(cwb, (1, 2, 0, 3)))


def _rope_tables(n_tokens):
    rows_n = n_tokens // GRID_W
    rows = jnp.repeat(jnp.arange(rows_n), GRID_W).astype(F32)
    cols = jnp.tile(jnp.arange(GRID_W), rows_n).astype(F32)
    nf = HEAD_DIM // 4
    inv = ROPE_BASE ** (-jnp.arange(nf, dtype=F32) / nf)
    ar = rows[:, None] * inv
    ac = cols[:, None] * inv
    cos = jnp.concatenate([jnp.cos(ar), jnp.cos(ar), jnp.cos(ac), jnp.cos(ac)], axis=-1)
    sin = jnp.concatenate([-jnp.sin(ar), jnp.sin(ar), -jnp.sin(ac), jnp.sin(ac)], axis=-1)
    return jnp.tile(cos, (1, 128 // HEAD_DIM)), jnp.tile(sin, (1, 128 // HEAD_DIM))


def _pool_inv_counts(n_tokens):
    t = jnp.arange(n_tokens)[:, None]
    half = jnp.repeat(jnp.asarray(POOL_WINDOWS) // 2, POOL_GROUP)[None, :]
    cnt = jnp.minimum(t + half, n_tokens) - jnp.maximum(t - half, 0)
    return 1.0 / cnt.astype(F32)


def _tile_sizes(T):
    return dict(tm=min(T, 512), tg=min(T, 512), tf=min(T, 512))


def kernel(x, c, ctx, c_ctx, w_ada, b_ada, norm1_g, w_in, gla_w_dec, gla_b_dec, gla_norm_g, q_norm_g,
           k_norm_g, sink_logit, pool_w, pool_scale, w_out, norm2_g, w_up, conv_w, conv_b, w_down):
    B, S, D = x.shape
    L = ctx.shape[1]
    c_rows = jnp.zeros((8, D), F32).at[:B].set(c).at[B].set(c_ctx)
    mod_all = _modulation(c_rows, w_ada, b_ada).reshape(DEPTH, 8, 6, D)
    cos, sin = _rope_tables(S)
    tri = jnp.stack([jnp.tril(jnp.ones((GLA_CHUNK, GLA_CHUNK), BF16)),
                     jnp.triu(jnp.ones((GLA_CHUNK, GLA_CHUNK), BF16))])
    ones64 = _block_diag_ones(SWA_Q + SWA_KV, HEAD_DIM) * jnp.asarray(1.0 / HEAD_DIM, BF16)
    ones96 = _block_diag_ones(GLA_V, GLA_DV)
    icnt_lat, icnt_ctx = _pool_inv_counts(S), _pool_inv_counts(L)
    bd = jnp.asarray((np.arange(GLA_V) // GLA_DV)[:, None] == (np.arange(GLA_QKP) // GLA_DKP)[None, :], F32)
    lat, cx = _tile_sizes(S), _tile_sizes(L)
    lat_row = lambda b: b
    ctx_row = lambda b: B
    zero_state = jnp.zeros((B, 2, GLA_V, GLA_QKP), F32)

    for l in range(DEPTH):
        p = _pack_layer(w_in[l], gla_w_dec[l], gla_b_dec[l], q_norm_g[l], k_norm_g[l], pool_w[l],
                        w_out[l], conv_w[l], conv_b[l])
        mod = mod_all[l]
        g1 = norm1_g[l][None, :]
        g2 = norm2_g[l][None, :]
        gng = jnp.tile(gla_norm_g[l], GLA_HEADS)[None, :]
        pools = pool_scale[l][None, :]
        wup = w_up[l].astype(BF16)
        wdn = w_down[l].astype(BF16)
        shared = (g1, p["w"], p["wdec"], p["bdec"], tri, ones64, p["qkg"])
        update_ctx = l < DEPTH - 1

        (cqdf, ckif, ckef, cqdb, ckib, ckeb, cdecf, cdecb, cgv, cgg, csq, cskv, cpu) = _in_proj(
            ctx, mod, ctx_row, *shared, cos[:L], sin[:L], tm=cx["tm"], rope=False)
        co_f, co_b, st = _gla(cqdf, ckif, ckef, cqdb, ckib, ckeb, cdecf, cdecb, cgv, zero_state, bd,
                              tg=cx["tg"])

        (qdf, kif, kef, qdb, kib, keb, decf, decb, gv, gg, sq, skv, pu) = _in_proj(
            x, mod, lat_row, *shared, cos, sin, tm=lat["tm"], rope=True)
        o_f, o_b, _ = _gla(qdf, kif, kef, qdb, kib, keb, decf, decb, gv, st, bd, tg=lat["tg"])
        swa = _swa(sq, skv, cskv, sink_logit[l], local=True)
        x1, h2 = _out_proj(x, o_f, o_b, gg, swa, pu, icnt_lat, mod, lat_row, gng, ones96, p["poolw"],
                           pools, p["wout"], g2, tm=lat["tm"])
        x = _ffn(h2, x1, mod, lat_row, wup, p["cwb"], wdn, tm=lat["tf"])

        if update_ctx:
            cswa = _swa(csq, None, cskv, sink_logit[l], local=False)
            c1, ch2 = _out_proj(ctx, co_f, co_b, cgg, cswa, cpu, icnt_ctx, mod, ctx_row, gng, ones96,
                                p["poolw"], pools, p["wout"], g2, tm=cx["tm"])
            ctx = _ffn(ch2, c1, mod, ctx_row, wup, p["cwb"], wdn, tm=cx["tf"])
    return x
```

```python
import functools

import jax
import jax.numpy as jnp
import numpy as np
from jax import lax
from jax.experimental import pallas as pl
from jax.experimental.pallas import tpu as pltpu

F32 = jnp.float32
BF16 = jnp.bfloat16

D_MODEL = 1024
DEPTH = 2
GRID_W = 64
GLA_HEADS = 4
GLA_DK = 48
GLA_DKP = 64
GLA_DV = 96
GLA_RANK = 16
GLA_TAU = 16.0
GLA_CHUNK = 64
SWA_HEADS = 6
SWA_KV_HEADS = 2
SWA_GROUP = SWA_HEADS // SWA_KV_HEADS
HEAD_DIM = 64
WINDOW = 128
Q_BLOCK = 128
SWA_SUB = 2
ROPE_BASE = 10000.0
POOL_WINDOWS = (2, 4, 8, 16)
POOL_GROUP = 64
POOL_HALF_MAX = max(POOL_WINDOWS) // 2
POOL_ROWS = 128
D_FF = 2816
EPS = 1e-6
NEG_INF = -1e30
LOG2E = 1.4426950408889634

GLA_QKP = GLA_HEADS * GLA_DKP
GLA_V = GLA_HEADS * GLA_DV
SWA_Q = SWA_HEADS * HEAD_DIM
SWA_KV = SWA_KV_HEADS * HEAD_DIM
POOL_W = len(POOL_WINDOWS) * POOL_GROUP
Z_PAD = 128

OFF_Q = 0
OFF_K = OFF_Q + GLA_QKP
OFF_GV = OFF_K + GLA_QKP
OFF_GG = OFF_GV + GLA_V
OFF_Z = OFF_GG + GLA_V
OFF_SV = OFF_Z + Z_PAD
OFF_SQ = OFF_SV + SWA_KV
OFF_SK = OFF_SQ + SWA_Q
OFF_PU = OFF_SK + SWA_KV
IN_WP = OFF_PU + POOL_W

SWA_HEAD_ORDER = tuple(h for t in range(SWA_GROUP) for h in (t, t + SWA_GROUP))

FF_CHUNK = 256
N_FF_CHUNKS = D_FF // FF_CHUNK
FFN_HALO = 16
TOKEN_RESIDUES = 8
V7X_VMEM_LIMIT = 56 * 1024 * 1024

_NT = (((1,), (1,)), ((), ()))
_TN = (((0,), (0,)), ((), ()))


def _dot(a, b):
    return jnp.dot(a, b, preferred_element_type=F32)


def _params(n_grid, flags=None):
    return pltpu.CompilerParams(dimension_semantics=("arbitrary",) * n_grid,
                                vmem_limit_bytes=V7X_VMEM_LIMIT, flags=flags)


def _sigmoid(x):
    return 1.0 / (1.0 + jnp.exp(-x))


def _log_sigmoid(x):
    return jnp.minimum(x, 0.0) - jnp.log1p(jnp.exp(-jnp.abs(x)))


def _rms(x):
    return x * lax.rsqrt(jnp.mean(x * x, axis=-1, keepdims=True) + EPS)


def _mod_kernel(c_ref, w_ref, b_ref, o_ref):
    c = c_ref[...]
    act = (c * _sigmoid(c)).astype(BF16)
    o_ref[0] = _dot(act, w_ref[0].astype(BF16)) + b_ref[0]


def _modulation(c_rows, w_ada, b_ada):
    tn = 1536
    n = w_ada.shape[-1]
    return pl.pallas_call(
        _mod_kernel,
        grid=(DEPTH, n // tn),
        in_specs=[pl.BlockSpec((8, D_MODEL), lambda l, j: (0, 0)),
                  pl.BlockSpec((1, D_MODEL, tn), lambda l, j: (l, 0, j)),
                  pl.BlockSpec((1, 1, tn), lambda l, j: (l, 0, j))],
        out_specs=pl.BlockSpec((1, 8, tn), lambda l, j: (l, 0, j)),
        out_shape=jax.ShapeDtypeStruct((DEPTH, 8, n), F32),
        compiler_params=_params(2),
        name="adaln_mod",
    )(c_rows, w_ada, b_ada.reshape(DEPTH, 1, n))


def _in_proj_kernel(x_ref, mod_ref, g1_ref, w_ref, wdec_ref, bdec_ref, tri_ref, ones_ref, qkg_ref,
                    cos_ref, sin_ref,
                    qdf_ref, kif_ref, kef_ref, qdb_ref, kib_ref, keb_ref, decf_ref, decb_ref,
                    gv_ref, gg_ref, sq_ref, skv_ref, pu_ref, *, tm, rope):
    x = x_ref[0]
    sh1 = mod_ref[0, 0:1, :]
    sc1 = mod_ref[0, 1:2, :]
    h = _rms(x) * (g1_ref[...] * (1.0 + sc1)) + sh1
    hb = h.astype(BF16)

    qk = _dot(hb, w_ref[:, OFF_Q:OFF_GV])
    q = qk[:, :GLA_QKP] * (GLA_DK ** -0.5)
    k = qk[:, GLA_QKP:]
    zsv = _dot(hb, w_ref[:, OFF_Z:OFF_SQ])
    z = zsv[:, :Z_PAD]
    skv_ref[0, :, SWA_KV:] = zsv[:, Z_PAD:].astype(BF16)
    la = _log_sigmoid(_dot(z.astype(BF16), wdec_ref[...]) + bdec_ref[...]) * (LOG2E / GLA_TAU)
    la_hi = la.astype(BF16)
    la_lo = (la - la_hi.astype(F32)).astype(BF16)
    lower = tri_ref[0]
    upper = tri_ref[1]
    C = GLA_CHUNK
    for c in range(tm // C):
        r = slice(c * C, (c + 1) * C)
        bcf = _dot(lower, la_hi[r, :GLA_QKP]) + _dot(lower, la_lo[r, :GLA_QKP])
        bcb = _dot(upper, la_hi[r, GLA_QKP:]) + _dot(upper, la_lo[r, GLA_QKP:])
        blf = bcf[C - 1:C, :]
        blb = bcb[0:1, :]
        qc = q[r]
        kc = k[r]
        qdf_ref[0, r, :] = (qc * jnp.exp2(bcf)).astype(BF16)
        kif_ref[0, r, :] = (kc * jnp.exp2(-bcf)).astype(BF16)
        kef_ref[0, r, :] = (kc * jnp.exp2(blf - bcf)).astype(BF16)
        qdb_ref[0, r, :] = (qc * jnp.exp2(bcb)).astype(BF16)
        kib_ref[0, r, :] = (kc * jnp.exp2(-bcb)).astype(BF16)
        keb_ref[0, r, :] = (kc * jnp.exp2(blb - bcb)).astype(BF16)
        decf_ref[0, c] = jnp.exp2(blf)
        decb_ref[0, c] = jnp.exp2(blb)

    vg = _dot(hb, w_ref[:, OFF_GV:OFF_Z])
    gv_ref[0] = vg[:, :GLA_V].astype(BF16)
    gg_ref[0] = vg[:, GLA_V:].astype(BF16)

    sqk = _dot(hb, w_ref[:, OFF_SQ:OFF_PU])
    ss = _dot((sqk * sqk).astype(BF16), ones_ref[...])
    sqk = sqk * lax.rsqrt(ss + EPS) * qkg_ref[...]
    tiles = []
    for t in range((SWA_Q + SWA_KV) // 128):
        xt = sqk[:, t * 128:(t + 1) * 128]
        if rope:
            lane = lax.broadcasted_iota(jnp.int32, xt.shape, 1)
            first = (lane % 32) < 16
            partner = jnp.where(first, pltpu.roll(xt, 128 - 16, 1), pltpu.roll(xt, 16, 1))
            xt = xt * cos_ref[...] + partner * sin_ref[...]
        tiles.append(xt.astype(BF16))
    for t in range(SWA_Q // 128):
        sq_ref[0, :, t * 128:(t + 1) * 128] = tiles[t]
    skv_ref[0, :, :SWA_KV] = tiles[-1]

    pu_ref[0] = _dot(hb, w_ref[:, OFF_PU:IN_WP])


def _in_proj(x, mod, mod_row, g1, w, wdec, bdec, tri, ones64, qkg, cos, sin, *, tm, rope):
    B, T, D = x.shape
    nt = T // tm
    nc = T // GLA_CHUNK

    def tok(width, dtype):
        return (jax.ShapeDtypeStruct((B, T, width), dtype),
                pl.BlockSpec((1, tm, width), lambda i, b: (b, i, 0)))

    def const(arr):
        nd = arr.ndim
        return pl.BlockSpec(arr.shape, lambda i, b: (0,) * nd)

    dec = (jax.ShapeDtypeStruct((B, nc, 1, GLA_QKP), F32),
           pl.BlockSpec((1, tm // GLA_CHUNK, 1, GLA_QKP), lambda i, b: (b, i, 0, 0)))
    outs = [tok(GLA_QKP, BF16)] * 6 + [dec, dec] + [tok(GLA_V, BF16), tok(GLA_V, BF16),
                                                   tok(SWA_Q, BF16), tok(2 * SWA_KV, BF16),
                                                   tok(POOL_W, F32)]
    return pl.pallas_call(
        functools.partial(_in_proj_kernel, tm=tm, rope=rope),
        grid=(nt, B),
        in_specs=[pl.BlockSpec((1, tm, D), lambda i, b: (b, i, 0)),
                  pl.BlockSpec((1, 6, D), lambda i, b: (mod_row(b), 0, 0)),
                  const(g1), const(w), const(wdec), const(bdec), const(tri), const(ones64), const(qkg),
                  pl.BlockSpec((tm, 128), lambda i, b: (i, 0)),
                  pl.BlockSpec((tm, 128), lambda i, b: (i, 0))],
        out_specs=[o[1] for o in outs],
        out_shape=[o[0] for o in outs],
        compiler_params=_params(2),
        name="in_proj",
    )(x, mod, g1, w, wdec, bdec, tri, ones64, qkg, cos, sin)


def _gla_kernel(qdf_ref, kif_ref, kef_ref, vf_ref, decf_ref, qdb_ref, kib_ref, keb_ref, vb_ref, decb_ref,
                s0_ref, bd_ref, of_ref, ob_ref, sfin_ref, st_ref, *, tg):
    i = pl.program_id(1)
    nb = pl.num_programs(1)
    C = GLA_CHUNK
    ncb = tg // C

    @pl.when(i == 0)
    def _():
        st_ref[...] = s0_ref[0]

    HC = GLA_HEADS * C
    row_head = lax.broadcasted_iota(jnp.int32, (HC, GLA_QKP), 0) // C
    k_mask = row_head == lax.broadcasted_iota(jnp.int32, (HC, GLA_QKP), 1) // GLA_DKP
    vlane = lax.broadcasted_iota(jnp.int32, (HC, GLA_V), 1)
    vhead = sum((vlane >= h * GLA_DV).astype(jnp.int32) for h in range(1, GLA_HEADS))
    v_mask = lax.broadcasted_iota(jnp.int32, (HC, GLA_V), 0) // C == vhead
    row_i = lax.broadcasted_iota(jnp.int32, (C, HC), 0)
    col_j = lax.broadcasted_iota(jnp.int32, (C, HC), 1) % C
    bd = bd_ref[...]

    def chunk(d, qd_ref, ki_ref, ke_ref, v_ref, dec_ref, o_ref, c):
        r = slice(c * C, (c + 1) * C)
        qd = qd_ref[0, r, :]
        ki = ki_ref[0, r, :]
        ke = ke_ref[0, r, :]
        vv = v_ref[0, r, :]
        k_bd = jnp.where(k_mask, jnp.concatenate([ki] * GLA_HEADS, axis=0), jnp.zeros((), BF16))
        v_bd = jnp.where(v_mask, jnp.concatenate([vv] * GLA_HEADS, axis=0), jnp.zeros((), BF16))
        st = st_ref[d]
        rhs = jnp.concatenate([k_bd, st.astype(BF16)], axis=0)
        res = lax.dot_general(qd, rhs, _NT, preferred_element_type=F32)
        keep = (col_j <= row_i) if d == 0 else (col_j >= row_i)
        p = jnp.where(keep, res[:, :HC], 0.0).astype(BF16)
        o = res[:, HC:] + _dot(p, v_bd)
        o_ref[0, r, :] = o.astype(BF16)
        upd = lax.dot_general(vv, ke, _TN, preferred_element_type=F32)
        st_ref[d] = st * dec_ref[0, c] + upd * bd

    for c in range(ncb):
        chunk(0, qdf_ref, kif_ref, kef_ref, vf_ref, decf_ref, of_ref, c)
        chunk(1, qdb_ref, kib_ref, keb_ref, vb_ref, decb_ref, ob_ref, ncb - 1 - c)

    @pl.when(i == nb - 1)
    def _():
        sfin_ref[0] = st_ref[...]


def _gla(qdf, kif, kef, qdb, kib, keb, decf, decb, v, s0, bd, *, tg):
    B, T, _ = v.shape
    nb = T // tg
    ncb = tg // GLA_CHUNK

    def fwd(width):
        return pl.BlockSpec((1, tg, width), lambda b, i: (b, i, 0))

    def bwd(width):
        return pl.BlockSpec((1, tg, width), lambda b, i: (b, nb - 1 - i, 0))

    dec_f = pl.BlockSpec((1, ncb, 1, GLA_QKP), lambda b, i: (b, i, 0, 0))
    dec_b = pl.BlockSpec((1, ncb, 1, GLA_QKP), lambda b, i: (b, nb - 1 - i, 0, 0))
    st_spec = pl.BlockSpec((1, 2, GLA_V, GLA_QKP), lambda b, i: (b, 0, 0, 0))
    return pl.pallas_call(
        functools.partial(_gla_kernel, tg=tg),
        grid=(B, nb),
        in_specs=[fwd(GLA_QKP), fwd(GLA_QKP), fwd(GLA_QKP), fwd(GLA_V), dec_f,
                  bwd(GLA_QKP), bwd(GLA_QKP), bwd(GLA_QKP), bwd(GLA_V), dec_b,
                  st_spec, pl.BlockSpec(bd.shape, lambda b, i: (0, 0))],
        out_specs=[fwd(GLA_V), bwd(GLA_V), st_spec],
        out_shape=[jax.ShapeDtypeStruct((B, T, GLA_V), BF16),
                   jax.ShapeDtypeStruct((B, T, GLA_V), BF16),
                   jax.ShapeDtypeStruct((B, 2, GLA_V, GLA_QKP), F32)],
        scratch_shapes=[pltpu.VMEM((2, GLA_V, GLA_QKP), F32)],
        compiler_params=_params(2),
        name="gla",
    )(qdf, kif, kef, v, decf, qdb, kib, keb, v, decb, s0, bd)


def _swa_kernel(*refs, n_local, seq_len):
    if n_local:
        q_ref, *loc_refs, ckv_ref, sink_ref, o_ref = refs
    else:
        q_ref, ckv_ref, sink_ref, o_ref = refs
        loc_refs = []
    QB = Q_BLOCK
    lane = lax.broadcasted_iota(jnp.int32, (QB, 128), 1)
    low = lane < HEAD_DIM
    n_loc = n_local * QB
    for sub in range(SWA_SUB):
        _swa_block(sub, q_ref, loc_refs[sub:sub + n_local] + [ckv_ref], sink_ref, o_ref, low,
                   n_local=n_local, n_loc=n_loc, seq_len=seq_len)


def _swa_block(sub, q_ref, kv_parts, sink_ref, o_ref, low, *, n_local, n_loc, seq_len):
    QB = Q_BLOCK
    n = pl.program_id(1) * SWA_SUB + sub
    rows = slice(sub * QB, (sub + 1) * QB)
    kcat = jnp.concatenate([r[0, :, :SWA_KV] for r in kv_parts], axis=0)
    vcat = jnp.concatenate([r[0, :, SWA_KV:] for r in kv_parts], axis=0)
    if n_local:
        qi = lax.broadcasted_iota(jnp.int32, (QB, n_loc), 0)
        kj = lax.broadcasted_iota(jnp.int32, (QB, n_loc), 1)
        k_pos = (n - 1) * QB + kj
        dist = kj - QB - qi
        valid = (dist >= -WINDOW) & (dist <= WINDOW) & (k_pos >= 0) & (k_pos < seq_len)

    def scores(g):
        slabs = []
        for t in range(SWA_GROUP):
            qt = q_ref[0, rows, t * 128:(t + 1) * 128]
            slabs.append(jnp.where(low if g == 0 else ~low, qt, jnp.zeros_like(qt)))
        return lax.dot_general(jnp.concatenate(slabs, axis=0), kcat, _NT,
                               preferred_element_type=F32)

    def attend(g, s_grp):
        ps, dens = [], []
        for t in range(SWA_GROUP):
            sink = sink_ref[g * SWA_GROUP + t] * LOG2E
            sh = s_grp[t * QB:(t + 1) * QB]
            s_ctx = sh[:, n_loc:]
            m = jnp.maximum(jnp.max(s_ctx, axis=-1, keepdims=True), sink)
            if n_local:
                s_loc = jnp.where(valid, sh[:, :n_loc], NEG_INF)
                m = jnp.maximum(m, jnp.max(s_loc, axis=-1, keepdims=True))
                p_loc = jnp.exp2(s_loc - m)
            p_ctx = jnp.exp2(s_ctx - m)
            den = jnp.sum(p_ctx, axis=-1, keepdims=True) + jnp.exp2(sink - m)
            if n_local:
                den = den + jnp.sum(p_loc, axis=-1, keepdims=True)
                ps.append(jnp.concatenate([p_loc, p_ctx], axis=1).astype(BF16))
            else:
                ps.append(p_ctx.astype(BF16))
            dens.append(den)
        o_grp = _dot(jnp.concatenate(ps, axis=0), vcat)
        return [o_grp[t * QB:(t + 1) * QB] / dens[t] for t in range(SWA_GROUP)]

    s_groups = [scores(g) for g in range(SWA_KV_HEADS)]
    o_a = attend(0, s_groups[0])
    o_b = attend(1, s_groups[1])
    for t in range(SWA_GROUP):
        o_ref[0, rows, t * 128:(t + 1) * 128] = jnp.where(low, o_a[t], o_b[t]).astype(BF16)


def _swa(q, kv, ckv, sink, *, local):
    B, S, _ = q.shape
    L = ckv.shape[1]
    nq = S // Q_BLOCK
    qspec = pl.BlockSpec((1, SWA_SUB * Q_BLOCK, SWA_Q), lambda b, n: (b, n, 0))
    cspec = pl.BlockSpec((1, L, 2 * SWA_KV), lambda b, n: (b, 0, 0))
    sspec = pl.BlockSpec(memory_space=pltpu.SMEM)
    if local:
        def blk(off):
            return pl.BlockSpec((1, Q_BLOCK, 2 * SWA_KV),
                                lambda b, n: (b, jnp.clip(n * SWA_SUB + off, 0, nq - 1), 0))
        offs = range(-1, SWA_SUB + 1)
        in_specs = [qspec] + [blk(off) for off in offs] + [cspec, sspec]
        args = (q,) + (kv,) * len(offs) + (ckv, sink)
    else:
        in_specs = [qspec, cspec, sspec]
        args = (q, ckv, sink)
    return pl.pallas_call(
        functools.partial(_swa_kernel, n_local=3 if local else 0, seq_len=S),
        grid=(B, nq // SWA_SUB),
        in_specs=in_specs,
        out_specs=qspec,
        out_shape=jax.ShapeDtypeStruct((B, S, SWA_Q), BF16),
        compiler_params=_params(2),
        name="swa",
    )(*args)


def _out_kernel(x_ref, of_ref, ob_ref, gg_ref, swa_ref, pu_ref, pup_ref, pun_ref, icnt_ref, mod_ref,
                gng_ref, ones_ref, poolw_ref, pools_ref, wout_ref, g2_ref,
                x1_ref, h2_ref, ext_ref, cen_ref, hs_ref, *, tm):
    i = pl.program_id(0)
    nt = pl.num_programs(0)
    HP = POOL_HALF_MAX

    o = of_ref[0].astype(F32) + ob_ref[0].astype(F32)
    ss = _dot((o * o).astype(BF16), ones_ref[...])
    gg = gg_ref[0].astype(F32)
    gla = o * lax.rsqrt(ss * (1.0 / GLA_DV) + EPS) * gng_ref[...] * (gg * _sigmoid(gg))

    ext_ref[0:HP, :] = jnp.where(i > 0, pup_ref[0], 0.0)
    ext_ref[HP:HP + tm, :] = pu_ref[0]
    ext_ref[HP + tm:, :] = jnp.where(i < nt - 1, pun_ref[0], 0.0)
    upper_group = lax.broadcasted_iota(jnp.int32, (POOL_ROWS, 128), 1) >= POOL_GROUP
    for rb in range(tm // POOL_ROWS):
        r0 = HP + rb * POOL_ROWS
        rows = slice(rb * POOL_ROWS, (rb + 1) * POOL_ROWS)
        for tile in range(POOL_W // 128):
            ls = slice(tile * 128, (tile + 1) * 128)
            n_rows = POOL_ROWS + 2 * HP
            run = ext_ref[r0 - HP:r0 + POOL_ROWS + HP, ls]
            wins = []
            span = 1
            for w in POOL_WINDOWS:
                while span < w:
                    run = run + pltpu.roll(run, span, 0)
                    span *= 2
                if w in POOL_WINDOWS[2 * tile:2 * tile + 2]:
                    lead = w // 2 - 1
                    aligned = pltpu.roll(run, n_rows - lead, 0) if lead else run
                    wins.append(aligned[HP:HP + POOL_ROWS])
                if len(wins) == 2:
                    break
            win = jnp.where(upper_group, wins[1], wins[0])
            centred = win * icnt_ref[rows, ls] - ext_ref[r0:r0 + POOL_ROWS, ls]
            cen_ref[rows, ls] = centred.astype(BF16)
    pool = _dot(cen_ref[...], poolw_ref[...]) * pools_ref[...]

    cat = jnp.concatenate([gla.astype(BF16), swa_ref[0], pool.astype(BF16)], axis=1)
    y = _dot(cat, wout_ref[...])
    g1 = mod_ref[0, 2:3, :]
    sh2 = mod_ref[0, 3:4, :]
    sc2 = mod_ref[0, 4:5, :]
    x1 = x_ref[0] + g1 * y
    x1_ref[0] = x1
    h2 = _rms(x1) * (g2_ref[...] * (1.0 + sc2)) + sh2
    J = tm // TOKEN_RESIDUES
    for lt in range(D_MODEL // 128):
        ls = slice(lt * 128, (lt + 1) * 128)
        hs_ref[lt] = h2[:, ls]
        for k in range(TOKEN_RESIDUES):
            h2_ref[0, k * J:(k + 1) * J, ls] = hs_ref[lt, pl.ds(k, J, stride=TOKEN_RESIDUES), :].astype(BF16)


def _out_proj(x, o_f, o_b, gg, swa, pu, icnt, mod, mod_row, gng, ones96, poolw, pools, wout, g2, *, tm):
    B, T, D = x.shape
    nt = T // tm
    HP = POOL_HALF_MAX
    nh = T // HP

    def tok(width):
        return pl.BlockSpec((1, tm, width), lambda i, b: (b, i, 0))

    def const(arr):
        nd = arr.ndim
        return pl.BlockSpec(arr.shape, lambda i, b: (0,) * nd)

    prev = pl.BlockSpec((1, HP, POOL_W), lambda i, b: (b, jnp.maximum(i * (tm // HP) - 1, 0), 0))
    nxt = pl.BlockSpec((1, HP, POOL_W), lambda i, b: (b, jnp.minimum((i + 1) * (tm // HP), nh - 1), 0))
    return pl.pallas_call(
        functools.partial(_out_kernel, tm=tm),
        grid=(nt, B),
        in_specs=[tok(D), tok(GLA_V), tok(GLA_V), tok(GLA_V), tok(SWA_Q), tok(POOL_W), prev, nxt,
                  pl.BlockSpec((tm, POOL_W), lambda i, b: (i, 0)),
                  pl.BlockSpec((1, 6, D), lambda i, b: (mod_row(b), 0, 0)),
                  const(gng), const(ones96), const(poolw), const(pools), const(wout), const(g2)],
        out_specs=[tok(D), tok(D)],
        out_shape=[jax.ShapeDtypeStruct((B, T, D), F32), jax.ShapeDtypeStruct((B, T, D), BF16)],
        scratch_shapes=[pltpu.VMEM((tm + 2 * HP, POOL_W), F32), pltpu.VMEM((tm, POOL_W), BF16),
                        pltpu.VMEM((D // 128, tm, 128), F32)],
        compiler_params=_params(2),
        name="out_proj",
    )(x, o_f, o_b, gg, swa, pu, pu, pu, icnt, mod, gng, ones96, poolw, pools, wout, g2)


def _ffn_kernel(h_ref, hp_ref, hn_ref, x_ref, mod_ref, wup_ref, cwb_ref, wdn_ref,
                o_ref, hx_ref, u_ref, act_ref, nat_ref, *, tm):
    i = pl.program_id(0)
    nt = pl.num_programs(0)
    H = FFN_HALO
    N = N_FF_CHUNKS
    K = TOKEN_RESIDUES
    J = tm // K
    hx_ref[0:tm, :] = h_ref[0]
    before = jnp.where(i > 0, hp_ref[0].astype(F32)[H - 1:H], 0.0)
    after = jnp.where(i < nt - 1, hn_ref[0].astype(F32)[0:1], 0.0)
    halo_row = lax.broadcasted_iota(jnp.int32, (H, D_MODEL), 0)
    hx_ref[tm:, :] = jnp.where(halo_row == 0, before, jnp.where(halo_row == 1, after, 0.0)).astype(BF16)
    row8 = lax.broadcasted_iota(jnp.int32, (8, 128), 0)

    def up(fc, slot):
        for part in range(2):
            c0 = part * D_FF + fc * FF_CHUNK
            u_ref[slot, part] = _dot(hx_ref[...], wup_ref[:, c0:c0 + FF_CHUNK])

    def conv_act(fc, slot):
        for lt in range(FF_CHUNK // 128):
            ls = slice(lt * 128, (lt + 1) * 128)
            wts = [cwb_ref[part, fc, :, ls] for part in range(2)]
            for k in range(K):
                taps = []
                for part in range(2):
                    u = u_ref.at[slot, part]
                    if k > 0:
                        prev = u[(k - 1) * J:k * J, ls]
                    else:
                        rolled = pltpu.roll(u[(K - 1) * J:K * J, ls], 1, 0)
                        head = jnp.where(row8 == 0, u[tm:tm + 1, ls], rolled[0:8])
                        prev = jnp.concatenate([head, rolled[8:]], axis=0)
                    if k < K - 1:
                        nxt = u[(k + 1) * J:(k + 2) * J, ls]
                    else:
                        rolled = pltpu.roll(u[0:J, ls], J - 1, 0)
                        tail = jnp.where(row8 == 7, u[tm + 1:tm + 2, ls], rolled[J - 8:J])
                        nxt = jnp.concatenate([rolled[:J - 8], tail], axis=0)
                    w = wts[part]
                    taps.append(prev * w[0:1] + u[k * J:(k + 1) * J, ls] * w[1:2] + nxt * w[2:3] + w[3:4])
                a, g = taps
                act_ref[fc, k * J:(k + 1) * J, ls] = ((g * _sigmoid(g)) * a).astype(BF16)

    up(0, 0)
    for s in range(1, N + 1):
        if s < N:
            up(s, s % 2)
        conv_act(s - 1, (s - 1) % 2)
    act = jnp.concatenate([act_ref[c] for c in range(N)], axis=1)
    y = mod_ref[0, 5:6, :] * _dot(act, wdn_ref[...])
    for lt in range(D_MODEL // 128):
        ls = slice(lt * 128, (lt + 1) * 128)
        for k in range(K):
            nat_ref[lt, pl.ds(k, J, stride=K), :] = y[k * J:(k + 1) * J, ls]
        o_ref[0, :, ls] = x_ref[0, :, ls] + nat_ref[lt]


def _ffn(h2, x1, mod, mod_row, wup, cwb, wdn, *, tm):
    B, T, D = x1.shape
    nt = T // tm
    H = FFN_HALO
    nh = T // H

    def tok(width):
        return pl.BlockSpec((1, tm, width), lambda i, b: (b, i, 0))

    def const(arr):
        nd = arr.ndim
        return pl.BlockSpec(arr.shape, lambda i, b: (0,) * nd)

    prev = pl.BlockSpec((1, H, D), lambda i, b: (b, jnp.maximum(i * (tm // H) - 1, 0), 0))
    nxt = pl.BlockSpec((1, H, D), lambda i, b: (b, jnp.minimum((i + 1) * (tm // H), nh - 1), 0))
    return pl.pallas_call(
        functools.partial(_ffn_kernel, tm=tm),
        grid=(nt, B),
        in_specs=[tok(D), prev, nxt, tok(D),
                  pl.BlockSpec((1, 6, D), lambda i, b: (mod_row(b), 0, 0)),
                  const(wup), const(cwb), const(wdn)],
        out_specs=tok(D),
        out_shape=jax.ShapeDtypeStruct((B, T, D), F32),
        scratch_shapes=[pltpu.VMEM((tm + H, D), BF16),
                        pltpu.VMEM((2, 2, tm + H, FF_CHUNK), F32),
                        pltpu.VMEM((N_FF_CHUNKS, tm, FF_CHUNK), BF16),
                        pltpu.VMEM((D // 128, tm, 128), F32)],
        compiler_params=_params(2),
        name="ffn",
    )(h2, h2, h2, x1, mod, wup, cwb, wdn)


def _pad_heads(a, n_heads, width, padded):
    a = a.reshape(a.shape[:-1] + (n_heads, width))
    a = jnp.pad(a, [(0, 0)] * (a.ndim - 1) + [(0, padded - width)])
    return a.reshape(a.shape[:-2] + (n_heads * padded,))


def _permute_heads(a, axis):
    shape = a.shape
    a = a.reshape(shape[:axis] + (SWA_HEADS, HEAD_DIM) + shape[axis + 1:])
    a = jnp.take(a, jnp.array(SWA_HEAD_ORDER), axis=axis)
    return a.reshape(shape)


def _block_diag_ones(n, block):
    idx = np.arange(n) // block
    return jnp.asarray(idx[:, None] == idx[None, :], BF16)


def _pack_layer(w_in, gla_w_dec, gla_b_dec, q_norm_g, k_norm_g, pool_w, w_out, conv_w, conv_b):
    gq, gk, gv, gg, zf, zb, aq, ak, av, pu = jnp.split(
        w_in, np.cumsum([192, 192, 384, 384, 16, 16, 384, 128, 128])[:].tolist(), axis=-1)
    z = jnp.pad(jnp.concatenate([zf, zb], axis=-1), ((0, 0), (0, Z_PAD - 2 * GLA_RANK)))
    w = jnp.concatenate([_pad_heads(gq, GLA_HEADS, GLA_DK, GLA_DKP),
                         _pad_heads(gk, GLA_HEADS, GLA_DK, GLA_DKP),
                         gv, gg, z, av, _permute_heads(aq, 1), ak, pu], axis=-1).astype(BF16)
    wdec = jnp.zeros((Z_PAD, 2 * GLA_QKP), F32)
    wdec = wdec.at[0:GLA_RANK, :GLA_QKP].set(_pad_heads(gla_w_dec[0], GLA_HEADS, GLA_DK, GLA_DKP))
    wdec = wdec.at[GLA_RANK:2 * GLA_RANK, GLA_QKP:].set(_pad_heads(gla_w_dec[1], GLA_HEADS, GLA_DK, GLA_DKP))
    bdec = jnp.concatenate([_pad_heads(gla_b_dec[0], GLA_HEADS, GLA_DK, GLA_DKP),
                            _pad_heads(gla_b_dec[1], GLA_HEADS, GLA_DK, GLA_DKP)])[None, :]
    qkg = jnp.concatenate([jnp.tile(q_norm_g, SWA_HEADS) * (HEAD_DIM ** -0.5 * LOG2E),
                           jnp.tile(k_norm_g, SWA_KV_HEADS)])[None, :]
    poolw = jax.scipy.linalg.block_diag(*[pool_w[g] for g in range(len(POOL_WINDOWS))]).astype(BF16)
    wout = jnp.concatenate([w_out[:GLA_V], _permute_heads(w_out[GLA_V:GLA_V + SWA_Q], 0),
                            w_out[GLA_V + SWA_Q:]], axis=0).astype(BF16)
    cwb = jnp.concatenate([conv_w, conv_b[None, :]], axis=0).reshape(4, 2, N_FF_CHUNKS, FF_CHUNK)
    return dict(w=w, wdec=wdec.astype(BF16), bdec=bdec, qkg=qkg, poolw=poolw, wout=wout,
                cwb=jnp.transpose(cwb, (1, 2, 0, 3)))


def _rope_tables(n_tokens):
    rows_n = n_tokens // GRID_W
    rows = jnp.repeat(jnp.arange(rows_n), GRID_W).astype(F32)
    cols = jnp.tile(jnp.arange(GRID_W), rows_n).astype(F32)
    nf = HEAD_DIM // 4
    inv = ROPE_BASE ** (-jnp.arange(nf, dtype=F32) / nf)
    ar = rows[:, None] * inv
    ac = cols[:, None] * inv
    cos = jnp.concatenate([jnp.cos(ar), jnp.cos(ar), jnp.cos(ac), jnp.cos(ac)], axis=-1)
    sin = jnp.concatenate([-jnp.sin(ar), jnp.sin(ar), -jnp.sin(ac), jnp.sin(ac)], axis=-1)
    return jnp.tile(cos, (1, 128 // HEAD_DIM)), jnp.tile(sin, (1, 128 // HEAD_DIM))


def _pool_inv_counts(n_tokens):
    t = jnp.arange(n_tokens)[:, None]
    half = jnp.repeat(jnp.asarray(POOL_WINDOWS) // 2, POOL_GROUP)[None, :]
    cnt = jnp.minimum(t + half, n_tokens) - jnp.maximum(t - half, 0)
    return 1.0 / cnt.astype(F32)


def _tile_sizes(T):
    return dict(tm=min(T, 512), tg=min(T, 512), tf=min(T, 512))


def kernel(x, c, ctx, c_ctx, w_ada, b_ada, norm1_g, w_in, gla_w_dec, gla_b_dec, gla_norm_g, q_norm_g,
           k_norm_g, sink_logit, pool_w, pool_scale, w_out, norm2_g, w_up, conv_w, conv_b, w_down):
    B, S, D = x.shape
    L = ctx.shape[1]
    c_rows = jnp.zeros((8, D), F32).at[:B].set(c).at[B].set(c_ctx)
    mod_all = _modulation(c_rows, w_ada, b_ada).reshape(DEPTH, 8, 6, D)
    cos, sin = _rope_tables(S)
    tri = jnp.stack([jnp.tril(jnp.ones((GLA_CHUNK, GLA_CHUNK), BF16)),
                     jnp.triu(jnp.ones((GLA_CHUNK, GLA_CHUNK), BF16))])
    ones64 = _block_diag_ones(SWA_Q + SWA_KV, HEAD_DIM) * jnp.asarray(1.0 / HEAD_DIM, BF16)
    ones96 = _block_diag_ones(GLA_V, GLA_DV)
    icnt_lat, icnt_ctx = _pool_inv_counts(S), _pool_inv_counts(L)
    bd = jnp.asarray((np.arange(GLA_V) // GLA_DV)[:, None] == (np.arange(GLA_QKP) // GLA_DKP)[None, :], F32)
    lat, cx = _tile_sizes(S), _tile_sizes(L)
    lat_row = lambda b: b
    ctx_row = lambda b: B
    zero_state = jnp.zeros((B, 2, GLA_V, GLA_QKP), F32)

    for l in range(DEPTH):
        p = _pack_layer(w_in[l], gla_w_dec[l], gla_b_dec[l], q_norm_g[l], k_norm_g[l], pool_w[l],
                        w_out[l], conv_w[l], conv_b[l])
        mod = mod_all[l]
        g1 = norm1_g[l][None, :]
        g2 = norm2_g[l][None, :]
        gng = jnp.tile(gla_norm_g[l], GLA_HEADS)[None, :]
        pools = pool_scale[l][None, :]
        wup = w_up[l].astype(BF16)
        wdn = w_down[l].astype(BF16)
        shared = (g1, p["w"], p["wdec"], p["bdec"], tri, ones64, p["qkg"])
        update_ctx = l < DEPTH - 1

        (cqdf, ckif, ckef, cqdb, ckib, ckeb, cdecf, cdecb, cgv, cgg, csq, cskv, cpu) = _in_proj(
            ctx, mod, ctx_row, *shared, cos[:L], sin[:L], tm=cx["tm"], rope=False)
        co_f, co_b, st = _gla(cqdf, ckif, ckef, cqdb, ckib, ckeb, cdecf, cdecb, cgv, zero_state, bd,
                              tg=cx["tg"])

        (qdf, kif, kef, qdb, kib, keb, decf, decb, gv, gg, sq, skv, pu) = _in_proj(
            x, mod, lat_row, *shared, cos, sin, tm=lat["tm"], rope=True)
        o_f, o_b, _ = _gla(qdf, kif, kef, qdb, kib, keb, decf, decb, gv, st, bd, tg=lat["tg"])
        swa = _swa(sq, skv, cskv, sink_logit[l], local=True)
        x1, h2 = _out_proj(x, o_f, o_b, gg, swa, pu, icnt_lat, mod, lat_row, gng, ones96, p["poolw"],
                           pools, p["wout"], g2, tm=lat["tm"])
        x = _ffn(h2, x1, mod, lat_row, wup, p["cwb"], wdn, tm=lat["tf"])

        if update_ctx:
            cswa = _swa(csq, None, cskv, sink_logit[l], local=False)
            c1, ch2 = _out_proj(ctx, co_f, co_b, cgg, cswa, cpu, icnt_ctx, mod, ctx_row, gng, ones96,
                                p["poolw"], pools, p["wout"], g2, tm=cx["tm"])
            ctx = _ffn(ch2, c1, mod, ctx_row, wup, p["cwb"], wdn, tm=cx["tf"])
    return x
```

```python
import functools

import jax
import jax.numpy as jnp
import numpy as np
from jax import lax
from jax.experimental import pallas as pl
from jax.experimental.pallas import tpu as pltpu

F32 = jnp.float32
BF16 = jnp.bfloat16

D_MODEL = 1024
DEPTH = 2
GRID_W = 64
GLA_HEADS = 4
GLA_DK = 48
GLA_DKP = 64
GLA_DV = 96
GLA_RANK = 16
GLA_TAU = 16.0
GLA_CHUNK = 64
SWA_HEADS = 6
SWA_KV_HEADS = 2
SWA_GROUP = SWA_HEADS // SWA_KV_HEADS
HEAD_DIM = 64
WINDOW = 128
Q_BLOCK = 128
SWA_SUB = 8
ROPE_BASE = 10000.0
POOL_WINDOWS = (2, 4, 8, 16)
POOL_GROUP = 64
POOL_HALF_MAX = max(POOL_WINDOWS) // 2
POOL_ROWS = 128
D_FF = 2816
EPS = 1e-6
NEG_INF = -1e30
LOG2E = 1.4426950408889634

GLA_QKP = GLA_HEADS * GLA_DKP
GLA_V = GLA_HEADS * GLA_DV
SWA_Q = SWA_HEADS * HEAD_DIM
SWA_KV = SWA_KV_HEADS * HEAD_DIM
POOL_W = len(POOL_WINDOWS) * POOL_GROUP
Z_PAD = 128

OFF_Q = 0
OFF_K = OFF_Q + GLA_QKP
OFF_GV = OFF_K + GLA_QKP
OFF_GG = OFF_GV + GLA_V
OFF_Z = OFF_GG + GLA_V
OFF_SV = OFF_Z + Z_PAD
OFF_SQ = OFF_SV + SWA_KV
OFF_SK = OFF_SQ + SWA_Q
OFF_PU = OFF_SK + SWA_KV
IN_WP = OFF_PU + POOL_W

SWA_HEAD_ORDER = tuple(h for t in range(SWA_GROUP) for h in (t, t + SWA_GROUP))

FF_CHUNK = 256
N_FF_CHUNKS = D_FF // FF_CHUNK
FFN_HALO = 16
TOKEN_RESIDUES = 8
V7X_VMEM_LIMIT = 56 * 1024 * 1024

_NT = (((1,), (1,)), ((), ()))
_TN = (((0,), (0,)), ((), ()))


def _dot(a, b):
    return jnp.dot(a, b, preferred_element_type=F32)


def _params(n_grid, flags=None):
    return pltpu.CompilerParams(dimension_semantics=("arbitrary",) * n_grid,
                                vmem_limit_bytes=V7X_VMEM_LIMIT, flags=flags)


def _sigmoid(x):
    return 1.0 / (1.0 + jnp.exp(-x))


def _log_sigmoid(x):
    return jnp.minimum(x, 0.0) - jnp.log(1.0 + jnp.exp(-jnp.abs(x)))


def _rms(x):
    return x * lax.rsqrt(jnp.mean(x * x, axis=-1, keepdims=True) + EPS)


def _mod_kernel(c_ref, w_ref, b_ref, o_ref):
    c = c_ref[...]
    act = (c * _sigmoid(c)).astype(BF16)
    o_ref[0] = _dot(act, w_ref[0].astype(BF16)) + b_ref[0]


def _modulation(c_rows, w_ada, b_ada):
    tn = 1536
    n = w_ada.shape[-1]
    return pl.pallas_call(
        _mod_kernel,
        grid=(DEPTH, n // tn),
        in_specs=[pl.BlockSpec((8, D_MODEL), lambda l, j: (0, 0)),
                  pl.BlockSpec((1, D_MODEL, tn), lambda l, j: (l, 0, j)),
                  pl.BlockSpec((1, 1, tn), lambda l, j: (l, 0, j))],
        out_specs=pl.BlockSpec((1, 8, tn), lambda l, j: (l, 0, j)),
        out_shape=jax.ShapeDtypeStruct((DEPTH, 8, n), F32),
        compiler_params=_params(2),
        name="adaln_mod",
    )(c_rows, w_ada, b_ada.reshape(DEPTH, 1, n))


def _in_proj_kernel(x_ref, mod_ref, g1_ref, w_ref, wdec_ref, bdec_ref, tri_ref, ones_ref, qkg_ref,
                    cos_ref, sin_ref,
                    qdf_ref, kif_ref, kef_ref, qdb_ref, kib_ref, keb_ref, decf_ref, decb_ref,
                    gv_ref, gg_ref, sq_ref, skv_ref, pu_ref, *, tm, rope):
    x = x_ref[0]
    sh1 = mod_ref[0, 0:1, :]
    sc1 = mod_ref[0, 1:2, :]
    h = _rms(x) * (g1_ref[...] * (1.0 + sc1)) + sh1
    hb = h.astype(BF16)

    qk = _dot(hb, w_ref[:, OFF_Q:OFF_GV])
    q = qk[:, :GLA_QKP] * (GLA_DK ** -0.5)
    k = qk[:, GLA_QKP:]
    zsv = _dot(hb, w_ref[:, OFF_Z:OFF_SQ])
    z = zsv[:, :Z_PAD]
    skv_ref[0, :, SWA_KV:] = zsv[:, Z_PAD:].astype(BF16)
    la = _log_sigmoid(_dot(z.astype(BF16), wdec_ref[...]) + bdec_ref[...]) * (LOG2E / GLA_TAU)
    la_hi = la.astype(BF16)
    la_lo = (la - la_hi.astype(F32)).astype(BF16)
    lower = tri_ref[0]
    upper = tri_ref[1]
    C = GLA_CHUNK
    for c in range(tm // C):
        r = slice(c * C, (c + 1) * C)
        bcf = _dot(lower, la_hi[r, :GLA_QKP]) + _dot(lower, la_lo[r, :GLA_QKP])
        bcb = _dot(upper, la_hi[r, GLA_QKP:]) + _dot(upper, la_lo[r, GLA_QKP:])
        blf = bcf[C - 1:C, :]
        blb = bcb[0:1, :]
        qc = q[r]
        kc = k[r]
        qdf_ref[0, r, :] = (qc * jnp.exp2(bcf)).astype(BF16)
        kif_ref[0, r, :] = (kc * jnp.exp2(-bcf)).astype(BF16)
        kef_ref[0, r, :] = (kc * jnp.exp2(blf - bcf)).astype(BF16)
        qdb_ref[0, r, :] = (qc * jnp.exp2(bcb)).astype(BF16)
        kib_ref[0, r, :] = (kc * jnp.exp2(-bcb)).astype(BF16)
        keb_ref[0, r, :] = (kc * jnp.exp2(blb - bcb)).astype(BF16)
        decf_ref[0, c] = jnp.exp2(blf)
        decb_ref[0, c] = jnp.exp2(blb)

    vg = _dot(hb, w_ref[:, OFF_GV:OFF_Z])
    gv_ref[0] = vg[:, :GLA_V].astype(BF16)
    gg_ref[0] = vg[:, GLA_V:].astype(BF16)

    sqk = _dot(hb, w_ref[:, OFF_SQ:OFF_PU])
    ss = _dot((sqk * sqk).astype(BF16), ones_ref[...])
    sqk = sqk * lax.rsqrt(ss + EPS) * qkg_ref[...]
    tiles = []
    for t in range((SWA_Q + SWA_KV) // 128):
        xt = sqk[:, t * 128:(t + 1) * 128]
        if rope:
            lane = lax.broadcasted_iota(jnp.int32, xt.shape, 1)
            first = (lane % 32) < 16
            partner = jnp.where(first, pltpu.roll(xt, 128 - 16, 1), pltpu.roll(xt, 16, 1))
            xt = xt * cos_ref[...] + partner * sin_ref[...]
        tiles.append(xt.astype(BF16))
    for t in range(SWA_Q // 128):
        sq_ref[0, :, t * 128:(t + 1) * 128] = tiles[t]
    skv_ref[0, :, :SWA_KV] = tiles[-1]

    pu_ref[0] = _dot(hb, w_ref[:, OFF_PU:IN_WP])


def _in_proj(x, mod, mod_row, g1, w, wdec, bdec, tri, ones64, qkg, cos, sin, *, tm, rope):
    B, T, D = x.shape
    nt = T // tm
    nc = T // GLA_CHUNK

    def tok(width, dtype):
        return (jax.ShapeDtypeStruct((B, T, width), dtype),
                pl.BlockSpec((1, tm, width), lambda i, b: (b, i, 0)))

    def const(arr):
        nd = arr.ndim
        return pl.BlockSpec(arr.shape, lambda i, b: (0,) * nd)

    dec = (jax.ShapeDtypeStruct((B, nc, 1, GLA_QKP), F32),
           pl.BlockSpec((1, tm // GLA_CHUNK, 1, GLA_QKP), lambda i, b: (b, i, 0, 0)))
    outs = [tok(GLA_QKP, BF16)] * 6 + [dec, dec] + [tok(GLA_V, BF16), tok(GLA_V, BF16),
                                                   tok(SWA_Q, BF16), tok(2 * SWA_KV, BF16),
                                                   tok(POOL_W, F32)]
    return pl.pallas_call(
        functools.partial(_in_proj_kernel, tm=tm, rope=rope),
        grid=(nt, B),
        in_specs=[pl.BlockSpec((1, tm, D), lambda i, b: (b, i, 0)),
                  pl.BlockSpec((1, 6, D), lambda i, b: (mod_row(b), 0, 0)),
                  const(g1), const(w), const(wdec), const(bdec), const(tri), const(ones64), const(qkg),
                  pl.BlockSpec((tm, 128), lambda i, b: (i, 0)),
                  pl.BlockSpec((tm, 128), lambda i, b: (i, 0))],
        out_specs=[o[1] for o in outs],
        out_shape=[o[0] for o in outs],
        compiler_params=_params(2),
        name="in_proj",
    )(x, mod, g1, w, wdec, bdec, tri, ones64, qkg, cos, sin)


def _gla_kernel(qdf_ref, kif_ref, kef_ref, vf_ref, decf_ref, qdb_ref, kib_ref, keb_ref, vb_ref, decb_ref,
                s0_ref, bd_ref, of_ref, ob_ref, sfin_ref, st_ref, *, tg):
    i = pl.program_id(1)
    nb = pl.num_programs(1)
    C = GLA_CHUNK
    ncb = tg // C

    @pl.when(i == 0)
    def _():
        st_ref[...] = s0_ref[0]

    HC = GLA_HEADS * C
    row_head = lax.broadcasted_iota(jnp.int32, (HC, GLA_QKP), 0) // C
    k_mask = row_head == lax.broadcasted_iota(jnp.int32, (HC, GLA_QKP), 1) // GLA_DKP
    vlane = lax.broadcasted_iota(jnp.int32, (HC, GLA_V), 1)
    vhead = sum((vlane >= h * GLA_DV).astype(jnp.int32) for h in range(1, GLA_HEADS))
    v_mask = lax.broadcasted_iota(jnp.int32, (HC, GLA_V), 0) // C == vhead
    row_i = lax.broadcasted_iota(jnp.int32, (C, HC), 0)
    col_j = lax.broadcasted_iota(jnp.int32, (C, HC), 1) % C
    bd = bd_ref[...]

    def chunk(d, qd_ref, ki_ref, ke_ref, v_ref, dec_ref, o_ref, c):
        r = slice(c * C, (c + 1) * C)
        qd = qd_ref[0, r, :]
        ki = ki_ref[0, r, :]
        ke = ke_ref[0, r, :]
        vv = v_ref[0, r, :]
        k_bd = jnp.where(k_mask, jnp.concatenate([ki] * GLA_HEADS, axis=0), jnp.zeros((), BF16))
        v_bd = jnp.where(v_mask, jnp.concatenate([vv] * GLA_HEADS, axis=0), jnp.zeros((), BF16))
        st = st_ref[d]
        rhs = jnp.concatenate([k_bd, st.astype(BF16)], axis=0)
        res = lax.dot_general(qd, rhs, _NT, preferred_element_type=F32)
        keep = (col_j <= row_i) if d == 0 else (col_j >= row_i)
        p = jnp.where(keep, res[:, :HC], 0.0).astype(BF16)
        o = res[:, HC:] + _dot(p, v_bd)
        o_ref[0, r, :] = o.astype(BF16)
        upd = lax.dot_general(vv, ke, _TN, preferred_element_type=F32)
        st_ref[d] = st * dec_ref[0, c] + upd * bd

    for c in range(ncb):
        chunk(0, qdf_ref, kif_ref, kef_ref, vf_ref, decf_ref, of_ref, c)
        chunk(1, qdb_ref, kib_ref, keb_ref, vb_ref, decb_ref, ob_ref, ncb - 1 - c)

    @pl.when(i == nb - 1)
    def _():
        sfin_ref[0] = st_ref[...]


def _gla(qdf, kif, kef, qdb, kib, keb, decf, decb, v, s0, bd, *, tg):
    B, T, _ = v.shape
    nb = T // tg
    ncb = tg // GLA_CHUNK

    def fwd(width):
        return pl.BlockSpec((1, tg, width), lambda b, i: (b, i, 0))

    def bwd(width):
        return pl.BlockSpec((1, tg, width), lambda b, i: (b, nb - 1 - i, 0))

    dec_f = pl.BlockSpec((1, ncb, 1, GLA_QKP), lambda b, i: (b, i, 0, 0))
    dec_b = pl.BlockSpec((1, ncb, 1, GLA_QKP), lambda b, i: (b, nb - 1 - i, 0, 0))
    st_spec = pl.BlockSpec((1, 2, GLA_V, GLA_QKP), lambda b, i: (b, 0, 0, 0))
    return pl.pallas_call(
        functools.partial(_gla_kernel, tg=tg),
        grid=(B, nb),
        in_specs=[fwd(GLA_QKP), fwd(GLA_QKP), fwd(GLA_QKP), fwd(GLA_V), dec_f,
                  bwd(GLA_QKP), bwd(GLA_QKP), bwd(GLA_QKP), bwd(GLA_V), dec_b,
                  st_spec, pl.BlockSpec(bd.shape, lambda b, i: (0, 0))],
        out_specs=[fwd(GLA_V), bwd(GLA_V), st_spec],
        out_shape=[jax.ShapeDtypeStruct((B, T, GLA_V), BF16),
                   jax.ShapeDtypeStruct((B, T, GLA_V), BF16),
                   jax.ShapeDtypeStruct((B, 2, GLA_V, GLA_QKP), F32)],
        scratch_shapes=[pltpu.VMEM((2, GLA_V, GLA_QKP), F32)],
        compiler_params=_params(2),
        name="gla",
    )(qdf, kif, kef, v, decf, qdb, kib, keb, v, decb, s0, bd)


def _swa_kernel(*refs, n_local, n_sub, seq_len):
    if n_local:
        q_ref, *loc_refs, ckv_ref, sink_ref, o_ref = refs
    else:
        q_ref, ckv_ref, sink_ref, o_ref = refs
        loc_refs = []
    QB = Q_BLOCK
    lane = lax.broadcasted_iota(jnp.int32, (QB, 128), 1)
    low = lane < HEAD_DIM
    n_loc = n_local * QB
    for sub in range(n_sub):
        _swa_block(pl.program_id(1) * n_sub + sub, sub, q_ref, loc_refs[sub:sub + n_local] + [ckv_ref],
                   sink_ref, o_ref, low, n_local=n_local, n_loc=n_loc, seq_len=seq_len)


def _swa_block(n, sub, q_ref, kv_parts, sink_ref, o_ref, low, *, n_local, n_loc, seq_len):
    QB = Q_BLOCK
    rows = slice(sub * QB, (sub + 1) * QB)
    kcat = jnp.concatenate([r[0, :, :SWA_KV] for r in kv_parts], axis=0)
    vcat = jnp.concatenate([r[0, :, SWA_KV:] for r in kv_parts], axis=0)
    if n_local:
        qi = lax.broadcasted_iota(jnp.int32, (QB, n_loc), 0)
        kj = lax.broadcasted_iota(jnp.int32, (QB, n_loc), 1)
        k_pos = (n - 1) * QB + kj
        dist = kj - QB - qi
        valid = (dist >= -WINDOW) & (dist <= WINDOW) & (k_pos >= 0) & (k_pos < seq_len)

    def scores(g):
        slabs = []
        for t in range(SWA_GROUP):
            qt = q_ref[0, rows, t * 128:(t + 1) * 128]
            slabs.append(jnp.where(low if g == 0 else ~low, qt, jnp.zeros_like(qt)))
        return lax.dot_general(jnp.concatenate(slabs, axis=0), kcat, _NT,
                               preferred_element_type=F32)

    def attend(g, s_grp):
        ps, dens = [], []
        for t in range(SWA_GROUP):
            sink = sink_ref[g * SWA_GROUP + t] * LOG2E
            sh = s_grp[t * QB:(t + 1) * QB]
            s_ctx = sh[:, n_loc:]
            m = jnp.maximum(jnp.max(s_ctx, axis=-1, keepdims=True), sink)
            if n_local:
                s_loc = jnp.where(valid, sh[:, :n_loc], NEG_INF)
                m = jnp.maximum(m, jnp.max(s_loc, axis=-1, keepdims=True))
                p_loc = jnp.exp2(s_loc - m)
            p_ctx = jnp.exp2(s_ctx - m)
            den = jnp.sum(p_ctx, axis=-1, keepdims=True) + jnp.exp2(sink - m)
            if n_local:
                den = den + jnp.sum(p_loc, axis=-1, keepdims=True)
                ps.append(jnp.concatenate([p_loc, p_ctx], axis=1).astype(BF16))
            else:
                ps.append(p_ctx.astype(BF16))
            dens.append(den)
        o_grp = _dot(jnp.concatenate(ps, axis=0), vcat)
        return [o_grp[t * QB:(t + 1) * QB] / dens[t] for t in range(SWA_GROUP)]

    s_groups = [scores(g) for g in range(SWA_KV_HEADS)]
    o_a = attend(0, s_groups[0])
    o_b = attend(1, s_groups[1])
    for t in range(SWA_GROUP):
        o_ref[0, rows, t * 128:(t + 1) * 128] = jnp.where(low, o_a[t], o_b[t]).astype(BF16)


def _swa(q, kv, ckv, sink, *, local):
    B, S, _ = q.shape
    L = ckv.shape[1]
    nq = S // Q_BLOCK
    n_sub = min(SWA_SUB, nq)
    qspec = pl.BlockSpec((1, n_sub * Q_BLOCK, SWA_Q), lambda b, n: (b, n, 0))
    cspec = pl.BlockSpec((1, L, 2 * SWA_KV), lambda b, n: (b, 0, 0))
    sspec = pl.BlockSpec(memory_space=pltpu.SMEM)
    if local:
        def blk(off):
            return pl.BlockSpec((1, Q_BLOCK, 2 * SWA_KV),
                                lambda b, n: (b, jnp.clip(n * n_sub + off, 0, nq - 1), 0))
        offs = range(-1, n_sub + 1)
        in_specs = [qspec] + [blk(off) for off in offs] + [cspec, sspec]
        args = (q,) + (kv,) * len(offs) + (ckv, sink)
    else:
        in_specs = [qspec, cspec, sspec]
        args = (q, ckv, sink)
    return pl.pallas_call(
        functools.partial(_swa_kernel, n_local=3 if local else 0, n_sub=n_sub, seq_len=S),
        grid=(B, nq // n_sub),
        in_specs=in_specs,
        out_specs=qspec,
        out_shape=jax.ShapeDtypeStruct((B, S, SWA_Q), BF16),
        compiler_params=_params(2),
        name="swa",
    )(*args)


def _out_kernel(x_ref, of_ref, ob_ref, gg_ref, swa_ref, pu_ref, pup_ref, pun_ref, icnt_ref, mod_ref,
                gng_ref, ones_ref, poolw_ref, pools_ref, wout_ref, g2_ref,
                x1_ref, h2_ref, ext_ref, cen_ref, hs_ref, *, tm):
    i = pl.program_id(0)
    nt = pl.num_programs(0)
    HP = POOL_HALF_MAX

    o = of_ref[0].astype(F32) + ob_ref[0].astype(F32)
    ss = _dot((o * o).astype(BF16), ones_ref[...])
    gg = gg_ref[0].astype(F32)
    gla = o * lax.rsqrt(ss * (1.0 / GLA_DV) + EPS) * gng_ref[...] * (gg * _sigmoid(gg))

    ext_ref[0:HP, :] = jnp.where(i > 0, pup_ref[0], 0.0)
    ext_ref[HP:HP + tm, :] = pu_ref[0]
    ext_ref[HP + tm:, :] = jnp.where(i < nt - 1, pun_ref[0], 0.0)
    upper_group = lax.broadcasted_iota(jnp.int32, (POOL_ROWS, 128), 1) >= POOL_GROUP
    for rb in range(tm // POOL_ROWS):
        r0 = HP + rb * POOL_ROWS
        rows = slice(rb * POOL_ROWS, (rb + 1) * POOL_ROWS)
        for tile in range(POOL_W // 128):
            ls = slice(tile * 128, (tile + 1) * 128)
            n_rows = POOL_ROWS + 2 * HP
            run = ext_ref[r0 - HP:r0 + POOL_ROWS + HP, ls]
            wins = []
            span = 1
            for w in POOL_WINDOWS:
                while span < w:
                    run = run + pltpu.roll(run, span, 0)
                    span *= 2
                if w in POOL_WINDOWS[2 * tile:2 * tile + 2]:
                    lead = w // 2 - 1
                    aligned = pltpu.roll(run, n_rows - lead, 0) if lead else run
                    wins.append(aligned[HP:HP + POOL_ROWS])
                if len(wins) == 2:
                    break
            win = jnp.where(upper_group, wins[1], wins[0])
            centred = win * icnt_ref[rows, ls] - ext_ref[r0:r0 + POOL_ROWS, ls]
            cen_ref[rows, ls] = centred.astype(BF16)
    pool = _dot(cen_ref[...], poolw_ref[...]) * pools_ref[...]

    cat = jnp.concatenate([gla.astype(BF16), swa_ref[0], pool.astype(BF16)], axis=1)
    y = _dot(cat, wout_ref[...])
    g1 = mod_ref[0, 2:3, :]
    sh2 = mod_ref[0, 3:4, :]
    sc2 = mod_ref[0, 4:5, :]
    x1 = x_ref[0] + g1 * y
    x1_ref[0] = x1
    h2 = _rms(x1) * (g2_ref[...] * (1.0 + sc2)) + sh2
    J = tm // TOKEN_RESIDUES
    for lt in range(D_MODEL // 128):
        ls = slice(lt * 128, (lt + 1) * 128)
        hs_ref[lt] = h2[:, ls]
        for k in range(TOKEN_RESIDUES):
            h2_ref[0, k * J:(k + 1) * J, ls] = hs_ref[lt, pl.ds(k, J, stride=TOKEN_RESIDUES), :].astype(BF16)


def _out_proj(x, o_f, o_b, gg, swa, pu, icnt, mod, mod_row, gng, ones96, poolw, pools, wout, g2, *, tm):
    B, T, D = x.shape
    nt = T // tm
    HP = POOL_HALF_MAX
    nh = T // HP

    def tok(width):
        return pl.BlockSpec((1, tm, width), lambda i, b: (b, i, 0))

    def const(arr):
        nd = arr.ndim
        return pl.BlockSpec(arr.shape, lambda i, b: (0,) * nd)

    prev = pl.BlockSpec((1, HP, POOL_W), lambda i, b: (b, jnp.maximum(i * (tm // HP) - 1, 0), 0))
    nxt = pl.BlockSpec((1, HP, POOL_W), lambda i, b: (b, jnp.minimum((i + 1) * (tm // HP), nh - 1), 0))
    return pl.pallas_call(
        functools.partial(_out_kernel, tm=tm),
        grid=(nt, B),
        in_specs=[tok(D), tok(GLA_V), tok(GLA_V), tok(GLA_V), tok(SWA_Q), tok(POOL_W), prev, nxt,
                  pl.BlockSpec((tm, POOL_W), lambda i, b: (i, 0)),
                  pl.BlockSpec((1, 6, D), lambda i, b: (mod_row(b), 0, 0)),
                  const(gng), const(ones96), const(poolw), const(pools), const(wout), const(g2)],
        out_specs=[tok(D), tok(D)],
        out_shape=[jax.ShapeDtypeStruct((B, T, D), F32), jax.ShapeDtypeStruct((B, T, D), BF16)],
        scratch_shapes=[pltpu.VMEM((tm + 2 * HP, POOL_W), F32), pltpu.VMEM((tm, POOL_W), BF16),
                        pltpu.VMEM((D // 128, tm, 128), F32)],
        compiler_params=_params(2),
        name="out_proj",
    )(x, o_f, o_b, gg, swa, pu, pu, pu, icnt, mod, gng, ones96, poolw, pools, wout, g2)


def _ffn_kernel(h_ref, hp_ref, hn_ref, x_ref, mod_ref, wup_ref, cwb_ref, wdn_ref,
                o_ref, hx_ref, u_ref, act_ref, nat_ref, *, tm):
    i = pl.program_id(0)
    nt = pl.num_programs(0)
    H = FFN_HALO
    N = N_FF_CHUNKS
    K = TOKEN_RESIDUES
    J = tm // K
    hx_ref[0:tm, :] = h_ref[0]
    before = jnp.where(i > 0, hp_ref[0].astype(F32)[H - 1:H], 0.0)
    after = jnp.where(i < nt - 1, hn_ref[0].astype(F32)[0:1], 0.0)
    halo_row = lax.broadcasted_iota(jnp.int32, (H, D_MODEL), 0)
    hx_ref[tm:, :] = jnp.where(halo_row == 0, before, jnp.where(halo_row == 1, after, 0.0)).astype(BF16)
    row8 = lax.broadcasted_iota(jnp.int32, (8, 128), 0)

    def up(fc, slot):
        for part in range(2):
            c0 = part * D_FF + fc * FF_CHUNK
            u_ref[slot, part] = _dot(hx_ref[...], wup_ref[:, c0:c0 + FF_CHUNK])

    def conv_act(fc, slot):
        for lt in range(FF_CHUNK // 128):
            ls = slice(lt * 128, (lt + 1) * 128)
            wts = [cwb_ref[part, fc, :, ls] for part in range(2)]
            for k in range(K):
                taps = []
                for part in range(2):
                    u = u_ref.at[slot, part]
                    if k > 0:
                        prev = u[(k - 1) * J:k * J, ls]
                    else:
                        rolled = pltpu.roll(u[(K - 1) * J:K * J, ls], 1, 0)
                        head = jnp.where(row8 == 0, u[tm:tm + 1, ls], rolled[0:8])
                        prev = jnp.concatenate([head, rolled[8:]], axis=0)
                    if k < K - 1:
                        nxt = u[(k + 1) * J:(k + 2) * J, ls]
                    else:
                        rolled = pltpu.roll(u[0:J, ls], J - 1, 0)
                        tail = jnp.where(row8 == 7, u[tm + 1:tm + 2, ls], rolled[J - 8:J])
                        nxt = jnp.concatenate([rolled[:J - 8], tail], axis=0)
                    w = wts[part]
                    taps.append(prev * w[0:1] + u[k * J:(k + 1) * J, ls] * w[1:2] + nxt * w[2:3] + w[3:4])
                a, g = taps
                act_ref[fc, k * J:(k + 1) * J, ls] = ((g * _sigmoid(g)) * a).astype(BF16)

    up(0, 0)
    for s in range(1, N + 1):
        if s < N:
            up(s, s % 2)
        conv_act(s - 1, (s - 1) % 2)
    act = jnp.concatenate([act_ref[c] for c in range(N)], axis=1)
    y = mod_ref[0, 5:6, :] * _dot(act, wdn_ref[...])
    for lt in range(D_MODEL // 128):
        ls = slice(lt * 128, (lt + 1) * 128)
        for k in range(K):
            nat_ref[lt, pl.ds(k, J, stride=K), :] = y[k * J:(k + 1) * J, ls]
        o_ref[0, :, ls] = x_ref[0, :, ls] + nat_ref[lt]


def _ffn(h2, x1, mod, mod_row, wup, cwb, wdn, *, tm):
    B, T, D = x1.shape
    nt = T // tm
    H = FFN_HALO
    nh = T // H

    def tok(width):
        return pl.BlockSpec((1, tm, width), lambda i, b: (b, i, 0))

    def const(arr):
        nd = arr.ndim
        return pl.BlockSpec(arr.shape, lambda i, b: (0,) * nd)

    prev = pl.BlockSpec((1, H, D), lambda i, b: (b, jnp.maximum(i * (tm // H) - 1, 0), 0))
    nxt = pl.BlockSpec((1, H, D), lambda i, b: (b, jnp.minimum((i + 1) * (tm // H), nh - 1), 0))
    return pl.pallas_call(
        functools.partial(_ffn_kernel, tm=tm),
        grid=(nt, B),
        in_specs=[tok(D), prev, nxt, tok(D),
                  pl.BlockSpec((1, 6, D), lambda i, b: (mod_row(b), 0, 0)),
                  const(wup), const(cwb), const(wdn)],
        out_specs=tok(D),
        out_shape=jax.ShapeDtypeStruct((B, T, D), F32),
        scratch_shapes=[pltpu.VMEM((tm + H, D), BF16),
                        pltpu.VMEM((2, 2, tm + H, FF_CHUNK), F32),
                        pltpu.VMEM((N_FF_CHUNKS, tm, FF_CHUNK), BF16),
                        pltpu.VMEM((D // 128, tm, 128), F32)],
        compiler_params=_params(2),
        name="ffn",
    )(h2, h2, h2, x1, mod, wup, cwb, wdn)


def _pad_heads(a, n_heads, width, padded):
    a = a.reshape(a.shape[:-1] + (n_heads, width))
    a = jnp.pad(a, [(0, 0)] * (a.ndim - 1) + [(0, padded - width)])
    return a.reshape(a.shape[:-2] + (n_heads * padded,))


def _permute_heads(a, axis):
    shape = a.shape
    a = a.reshape(shape[:axis] + (SWA_HEADS, HEAD_DIM) + shape[axis + 1:])
    a = jnp.take(a, jnp.array(SWA_HEAD_ORDER), axis=axis)
    return a.reshape(shape)


def _block_diag_ones(n, block):
    idx = np.arange(n) // block
    return jnp.asarray(idx[:, None] == idx[None, :], BF16)


def _pack_layer(w_in, gla_w_dec, gla_b_dec, q_norm_g, k_norm_g, pool_w, w_out, conv_w, conv_b):
    gq, gk, gv, gg, zf, zb, aq, ak, av, pu = jnp.split(
        w_in, np.cumsum([192, 192, 384, 384, 16, 16, 384, 128, 128])[:].tolist(), axis=-1)
    z = jnp.pad(jnp.concatenate([zf, zb], axis=-1), ((0, 0), (0, Z_PAD - 2 * GLA_RANK)))
    w = jnp.concatenate([_pad_heads(gq, GLA_HEADS, GLA_DK, GLA_DKP),
                         _pad_heads(gk, GLA_HEADS, GLA_DK, GLA_DKP),
                         gv, gg, z, av, _permute_heads(aq, 1), ak, pu], axis=-1).astype(BF16)
    wdec = jnp.zeros((Z_PAD, 2 * GLA_QKP), F32)
    wdec = wdec.at[0:GLA_RANK, :GLA_QKP].set(_pad_heads(gla_w_dec[0], GLA_HEADS, GLA_DK, GLA_DKP))
    wdec = wdec.at[GLA_RANK:2 * GLA_RANK, GLA_QKP:].set(_pad_heads(gla_w_dec[1], GLA_HEADS, GLA_DK, GLA_DKP))
    bdec = jnp.concatenate([_pad_heads(gla_b_dec[0], GLA_HEADS, GLA_DK, GLA_DKP),
                            _pad_heads(gla_b_dec[1], GLA_HEADS, GLA_DK, GLA_DKP)])[None, :]
    qkg = jnp.concatenate([jnp.tile(q_norm_g, SWA_HEADS) * (HEAD_DIM ** -0.5 * LOG2E),
                           jnp.tile(k_norm_g, SWA_KV_HEADS)])[None, :]
    poolw = jax.scipy.linalg.block_diag(*[pool_w[g] for g in range(len(POOL_WINDOWS))]).astype(BF16)
    wout = jnp.concatenate([w_out[:GLA_V], _permute_heads(w_out[GLA_V:GLA_V + SWA_Q], 0),
                            w_out[GLA_V + SWA_Q:]], axis=0).astype(BF16)
    cwb = jnp.concatenate([conv_w, conv_b[None, :]], axis=0).reshape(4, 2, N_FF_CHUNKS, FF_CHUNK)
    return dict(w=w, wdec=wdec.astype(BF16), bdec=bdec, qkg=qkg, poolw=poolw, wout=wout,
                cwb=jnp.transpose(cwb, (1, 2, 0, 3)))


def _rope_tables(n_tokens):
    rows_n = n_tokens // GRID_W
    rows = jnp.repeat(jnp.arange(rows_n), GRID_W).astype(F32)
    cols = jnp.tile(jnp.arange(GRID_W), rows_n).astype(F32)
    nf = HEAD_DIM // 4
    inv = ROPE_BASE ** (-jnp.arange(nf, dtype=F32) / nf)
    ar = rows[:, None] * inv
    ac = cols[:, None] * inv
    cos = jnp.concatenate([jnp.cos(ar), jnp.cos(ar), jnp.cos(ac), jnp.cos(ac)], axis=-1)
    sin = jnp.concatenate([-jnp.sin(ar), jnp.sin(ar), -jnp.sin(ac), jnp.sin(ac)], axis=-1)
    return jnp.tile(cos, (1, 128 // HEAD_DIM)), jnp.tile(sin, (1, 128 // HEAD_DIM))


def _pool_inv_counts(n_tokens):
    t = jnp.arange(n_tokens)[:, None]
    half = jnp.repeat(jnp.asarray(POOL_WINDOWS) // 2, POOL_GROUP)[None, :]
    cnt = jnp.minimum(t + half, n_tokens) - jnp.maximum(t - half, 0)
    return 1.0 / cnt.astype(F32)


def _tile_sizes(T):
    return dict(tm=min(T, 512), tg=min(T, 512), tf=min(T, 512))


def kernel(x, c, ctx, c_ctx, w_ada, b_ada, norm1_g, w_in, gla_w_dec, gla_b_dec, gla_norm_g, q_norm_g,
           k_norm_g, sink_logit, pool_w, pool_scale, w_out, norm2_g, w_up, conv_w, conv_b, w_down):
    B, S, D = x.shape
    L = ctx.shape[1]
    c_rows = jnp.zeros((8, D), F32).at[:B].set(c).at[B].set(c_ctx)
    mod_all = _modulation(c_rows, w_ada, b_ada).reshape(DEPTH, 8, 6, D)
    cos, sin = _rope_tables(S)
    chunk_ones = np.ones((GLA_CHUNK, GLA_CHUNK), np.float32)
    tri = jnp.asarray(np.stack([np.tril(chunk_ones), np.triu(chunk_ones)]), BF16)
    ones64 = _block_diag_ones(SWA_Q + SWA_KV, HEAD_DIM) * jnp.asarray(1.0 / HEAD_DIM, BF16)
    ones96 = _block_diag_ones(GLA_V, GLA_DV)
    icnt_lat, icnt_ctx = _pool_inv_counts(S), _pool_inv_counts(L)
    bd = jnp.asarray((np.arange(GLA_V) // GLA_DV)[:, None] == (np.arange(GLA_QKP) // GLA_DKP)[None, :], F32)
    lat, cx = _tile_sizes(S), _tile_sizes(L)
    lat_row = lambda b: b
    ctx_row = lambda b: B
    zero_state = jnp.zeros((B, 2, GLA_V, GLA_QKP), F32)

    packed = jax.vmap(_pack_layer)(w_in, gla_w_dec, gla_b_dec, q_norm_g, k_norm_g, pool_w, w_out, conv_w,
                                   conv_b)
    w_up_bf, w_down_bf = w_up.astype(BF16), w_down.astype(BF16)

    for l in range(DEPTH):
        p = {name: arr[l] for name, arr in packed.items()}
        mod = mod_all[l]
        g1 = norm1_g[l][None, :]
        g2 = norm2_g[l][None, :]
        gng = jnp.tile(gla_norm_g[l], GLA_HEADS)[None, :]
        pools = pool_scale[l][None, :]
        wup = w_up_bf[l]
        wdn = w_down_bf[l]
        shared = (g1, p["w"], p["wdec"], p["bdec"], tri, ones64, p["qkg"])
        update_ctx = l < DEPTH - 1

        (cqdf, ckif, ckef, cqdb, ckib, ckeb, cdecf, cdecb, cgv, cgg, csq, cskv, cpu) = _in_proj(
            ctx, mod, ctx_row, *shared, cos[:L], sin[:L], tm=cx["tm"], rope=False)
        co_f, co_b, st = _gla(cqdf, ckif, ckef, cqdb, ckib, ckeb, cdecf, cdecb, cgv, zero_state, bd,
                              tg=cx["tg"])

        (qdf, kif, kef, qdb, kib, keb, decf, decb, gv, gg, sq, skv, pu) = _in_proj(
            x, mod, lat_row, *shared, cos, sin, tm=lat["tm"], rope=True)
        o_f, o_b, _ = _gla(qdf, kif, kef, qdb, kib, keb, decf, decb, gv, st, bd, tg=lat["tg"])
        swa = _swa(sq, skv, cskv, sink_logit[l], local=True)
        x1, h2 = _out_proj(x, o_f, o_b, gg, swa, pu, icnt_lat, mod, lat_row, gng, ones96, p["poolw"],
                           pools, p["wout"], g2, tm=lat["tm"])
        x = _ffn(h2, x1, mod, lat_row, wup, p["cwb"], wdn, tm=lat["tf"])

        if update_ctx:
            cswa = _swa(csq, None, cskv, sink_logit[l], local=False)
            c1, ch2 = _out_proj(ctx, co_f, co_b, cgg, cswa, cpu, icnt_ctx, mod, ctx_row, gng, ones96,
                                p["poolw"], pools, p["wout"], g2, tm=cx["tm"])
            ctx = _ffn(ch2, c1, mod, ctx_row, wup, p["cwb"], wdn, tm=cx["tf"])
    return x
```

```python
import functools

import jax
import jax.numpy as jnp
import numpy as np
from jax import lax
from jax.experimental import pallas as pl
from jax.experimental.pallas import tpu as pltpu

F32 = jnp.float32
BF16 = jnp.bfloat16

D_MODEL = 1024
DEPTH = 2
GRID_W = 64
GLA_HEADS = 4
GLA_DK = 48
GLA_DKP = 64
GLA_DV = 96
GLA_RANK = 16
GLA_TAU = 16.0
GLA_CHUNK = 64
SWA_HEADS = 6
SWA_KV_HEADS = 2
SWA_GROUP = SWA_HEADS // SWA_KV_HEADS
HEAD_DIM = 64
WINDOW = 128
Q_BLOCK = 128
SWA_SUB = 8
ROPE_BASE = 10000.0
POOL_WINDOWS = (2, 4, 8, 16)
POOL_GROUP = 64
POOL_HALF_MAX = max(POOL_WINDOWS) // 2
POOL_ROWS = 128
D_FF = 2816
EPS = 1e-6
NEG_INF = -1e30
LOG2E = 1.4426950408889634

GLA_QKP = GLA_HEADS * GLA_DKP
GLA_V = GLA_HEADS * GLA_DV
SWA_Q = SWA_HEADS * HEAD_DIM
SWA_KV = SWA_KV_HEADS * HEAD_DIM
POOL_W = len(POOL_WINDOWS) * POOL_GROUP
Z_PAD = 128

OFF_Q = 0
OFF_K = OFF_Q + GLA_QKP
OFF_GV = OFF_K + GLA_QKP
OFF_GG = OFF_GV + GLA_V
OFF_Z = OFF_GG + GLA_V
OFF_SV = OFF_Z + Z_PAD
OFF_SQ = OFF_SV + SWA_KV
OFF_SK = OFF_SQ + SWA_Q
OFF_PU = OFF_SK + SWA_KV
IN_WP = OFF_PU + POOL_W

SWA_HEAD_ORDER = tuple(h for t in range(SWA_GROUP) for h in (t, t + SWA_GROUP))

FF_CHUNK = 256
N_FF_CHUNKS = D_FF // FF_CHUNK
FFN_HALO = 16
TOKEN_RESIDUES = 8
V7X_VMEM_LIMIT = 56 * 1024 * 1024

_NT = (((1,), (1,)), ((), ()))
_TN = (((0,), (0,)), ((), ()))


def _dot(a, b):
    return jnp.dot(a, b, preferred_element_type=F32)


def _params(n_grid, flags=None):
    return pltpu.CompilerParams(dimension_semantics=("arbitrary",) * n_grid,
                                vmem_limit_bytes=V7X_VMEM_LIMIT, flags=flags)


def _sigmoid(x):
    return 1.0 / (1.0 + jnp.exp2(x * -LOG2E))


def _log_sigmoid(x):
    return jnp.minimum(x, 0.0) - jnp.log(1.0 + jnp.exp2(jnp.abs(x) * -LOG2E))


def _rms(x):
    return x * lax.rsqrt(jnp.mean(x * x, axis=-1, keepdims=True) + EPS)


def _mod_kernel(c_ref, w_ref, b_ref, o_ref):
    c = c_ref[...]
    act = (c * _sigmoid(c)).astype(BF16)
    o_ref[0] = _dot(act, w_ref[0].astype(BF16)) + b_ref[0]


def _modulation(c_rows, w_ada, b_ada):
    tn = 1536
    n = w_ada.shape[-1]
    return pl.pallas_call(
        _mod_kernel,
        grid=(DEPTH, n // tn),
        in_specs=[pl.BlockSpec((8, D_MODEL), lambda l, j: (0, 0)),
                  pl.BlockSpec((1, D_MODEL, tn), lambda l, j: (l, 0, j)),
                  pl.BlockSpec((1, 1, tn), lambda l, j: (l, 0, j))],
        out_specs=pl.BlockSpec((1, 8, tn), lambda l, j: (l, 0, j)),
        out_shape=jax.ShapeDtypeStruct((DEPTH, 8, n), F32),
        compiler_params=_params(2),
        name="adaln_mod",
    )(c_rows, w_ada, b_ada.reshape(DEPTH, 1, n))


def _in_proj_kernel(x_ref, mod_ref, g1_ref, w_ref, wdec_ref, bdec_ref, tri_ref, ones_ref, qkg_ref,
                    cos_ref, sin_ref,
                    qdf_ref, kif_ref, kef_ref, qdb_ref, kib_ref, keb_ref, decf_ref, decb_ref,
                    gv_ref, gg_ref, sq_ref, skv_ref, pu_ref, *, tm, rope):
    x = x_ref[0]
    sh1 = mod_ref[0, 0:1, :]
    sc1 = mod_ref[0, 1:2, :]
    h = _rms(x) * (g1_ref[...] * (1.0 + sc1)) + sh1
    hb = h.astype(BF16)

    qk = _dot(hb, w_ref[:, OFF_Q:OFF_GV])
    q = qk[:, :GLA_QKP] * (GLA_DK ** -0.5)
    k = qk[:, GLA_QKP:]
    zsv = _dot(hb, w_ref[:, OFF_Z:OFF_SQ])
    z = zsv[:, :Z_PAD]
    skv_ref[0, :, SWA_KV:] = zsv[:, Z_PAD:].astype(BF16)
    la = _log_sigmoid(_dot(z.astype(BF16), wdec_ref[...]) + bdec_ref[...]) * (LOG2E / GLA_TAU)
    la_hi = la.astype(BF16)
    la_lo = (la - la_hi.astype(F32)).astype(BF16)
    lower = tri_ref[0]
    upper = tri_ref[1]
    C = GLA_CHUNK
    for c in range(tm // C):
        r = slice(c * C, (c + 1) * C)
        bcf = _dot(lower, la_hi[r, :GLA_QKP]) + _dot(lower, la_lo[r, :GLA_QKP])
        bcb = _dot(upper, la_hi[r, GLA_QKP:]) + _dot(upper, la_lo[r, GLA_QKP:])
        blf = bcf[C - 1:C, :]
        blb = bcb[0:1, :]
        qc = q[r]
        kc = k[r]
        qdf_ref[0, r, :] = (qc * jnp.exp2(bcf)).astype(BF16)
        kif_ref[0, r, :] = (kc * jnp.exp2(-bcf)).astype(BF16)
        kef_ref[0, r, :] = (kc * jnp.exp2(blf - bcf)).astype(BF16)
        qdb_ref[0, r, :] = (qc * jnp.exp2(bcb)).astype(BF16)
        kib_ref[0, r, :] = (kc * jnp.exp2(-bcb)).astype(BF16)
        keb_ref[0, r, :] = (kc * jnp.exp2(blb - bcb)).astype(BF16)
        decf_ref[0, c] = jnp.exp2(blf)
        decb_ref[0, c] = jnp.exp2(blb)

    vg = _dot(hb, w_ref[:, OFF_GV:OFF_Z])
    gv_ref[0] = vg[:, :GLA_V].astype(BF16)
    gg_ref[0] = vg[:, GLA_V:].astype(BF16)

    sqk = _dot(hb, w_ref[:, OFF_SQ:OFF_PU])
    ss = _dot((sqk * sqk).astype(BF16), ones_ref[...])
    sqk = sqk * lax.rsqrt(ss + EPS) * qkg_ref[...]
    tiles = []
    for t in range((SWA_Q + SWA_KV) // 128):
        xt = sqk[:, t * 128:(t + 1) * 128]
        if rope:
            lane = lax.broadcasted_iota(jnp.int32, xt.shape, 1)
            first = (lane % 32) < 16
            partner = jnp.where(first, pltpu.roll(xt, 128 - 16, 1), pltpu.roll(xt, 16, 1))
            xt = xt * cos_ref[...] + partner * sin_ref[...]
        tiles.append(xt.astype(BF16))
    for t in range(SWA_Q // 128):
        sq_ref[0, :, t * 128:(t + 1) * 128] = tiles[t]
    skv_ref[0, :, :SWA_KV] = tiles[-1]

    pu_ref[0] = _dot(hb, w_ref[:, OFF_PU:IN_WP])


def _in_proj(x, mod, mod_row, g1, w, wdec, bdec, tri, ones64, qkg, cos, sin, *, tm, rope):
    B, T, D = x.shape
    nt = T // tm
    nc = T // GLA_CHUNK

    def tok(width, dtype):
        return (jax.ShapeDtypeStruct((B, T, width), dtype),
                pl.BlockSpec((1, tm, width), lambda i, b: (b, i, 0)))

    def const(arr):
        nd = arr.ndim
        return pl.BlockSpec(arr.shape, lambda i, b: (0,) * nd)

    dec = (jax.ShapeDtypeStruct((B, nc, 1, GLA_QKP), F32),
           pl.BlockSpec((1, tm // GLA_CHUNK, 1, GLA_QKP), lambda i, b: (b, i, 0, 0)))
    outs = [tok(GLA_QKP, BF16)] * 6 + [dec, dec] + [tok(GLA_V, BF16), tok(GLA_V, BF16),
                                                   tok(SWA_Q, BF16), tok(2 * SWA_KV, BF16),
                                                   tok(POOL_W, F32)]
    return pl.pallas_call(
        functools.partial(_in_proj_kernel, tm=tm, rope=rope),
        grid=(nt, B),
        in_specs=[pl.BlockSpec((1, tm, D), lambda i, b: (b, i, 0)),
                  pl.BlockSpec((1, 6, D), lambda i, b: (mod_row(b), 0, 0)),
                  const(g1), const(w), const(wdec), const(bdec), const(tri), const(ones64), const(qkg),
                  pl.BlockSpec((tm, 128), lambda i, b: (i, 0)),
                  pl.BlockSpec((tm, 128), lambda i, b: (i, 0))],
        out_specs=[o[1] for o in outs],
        out_shape=[o[0] for o in outs],
        compiler_params=_params(2),
        name="in_proj",
    )(x, mod, g1, w, wdec, bdec, tri, ones64, qkg, cos, sin)


def _gla_kernel(qdf_ref, kif_ref, kef_ref, vf_ref, decf_ref, qdb_ref, kib_ref, keb_ref, vb_ref, decb_ref,
                s0_ref, bd_ref, of_ref, ob_ref, sfin_ref, st_ref, *, tg):
    i = pl.program_id(1)
    nb = pl.num_programs(1)
    C = GLA_CHUNK
    ncb = tg // C

    @pl.when(i == 0)
    def _():
        st_ref[...] = s0_ref[0]

    HC = GLA_HEADS * C
    row_head = lax.broadcasted_iota(jnp.int32, (HC, GLA_QKP), 0) // C
    k_mask = row_head == lax.broadcasted_iota(jnp.int32, (HC, GLA_QKP), 1) // GLA_DKP
    vlane = lax.broadcasted_iota(jnp.int32, (HC, GLA_V), 1)
    vhead = sum((vlane >= h * GLA_DV).astype(jnp.int32) for h in range(1, GLA_HEADS))
    v_mask = lax.broadcasted_iota(jnp.int32, (HC, GLA_V), 0) // C == vhead
    row_i = lax.broadcasted_iota(jnp.int32, (C, HC), 0)
    col_j = lax.broadcasted_iota(jnp.int32, (C, HC), 1) % C
    bd = bd_ref[...]

    def chunk(d, qd_ref, ki_ref, ke_ref, v_ref, dec_ref, o_ref, c):
        r = slice(c * C, (c + 1) * C)
        qd = qd_ref[0, r, :]
        ki = ki_ref[0, r, :]
        ke = ke_ref[0, r, :]
        vv = v_ref[0, r, :]
        k_bd = jnp.where(k_mask, jnp.concatenate([ki] * GLA_HEADS, axis=0), jnp.zeros((), BF16))
        v_bd = jnp.where(v_mask, jnp.concatenate([vv] * GLA_HEADS, axis=0), jnp.zeros((), BF16))
        st = st_ref[d]
        rhs = jnp.concatenate([k_bd, st.astype(BF16)], axis=0)
        res = lax.dot_general(qd, rhs, _NT, preferred_element_type=F32)
        keep = (col_j <= row_i) if d == 0 else (col_j >= row_i)
        p = jnp.where(keep, res[:, :HC], 0.0).astype(BF16)
        o = res[:, HC:] + _dot(p, v_bd)
        o_ref[0, r, :] = o.astype(BF16)
        upd = lax.dot_general(vv, ke, _TN, preferred_element_type=F32)
        st_ref[d] = st * dec_ref[0, c] + upd * bd

    for c in range(ncb):
        chunk(0, qdf_ref, kif_ref, kef_ref, vf_ref, decf_ref, of_ref, c)
        chunk(1, qdb_ref, kib_ref, keb_ref, vb_ref, decb_ref, ob_ref, ncb - 1 - c)

    @pl.when(i == nb - 1)
    def _():
        sfin_ref[0] = st_ref[...]


def _gla(qdf, kif, kef, qdb, kib, keb, decf, decb, v, s0, bd, *, tg):
    B, T, _ = v.shape
    nb = T // tg
    ncb = tg // GLA_CHUNK

    def fwd(width):
        return pl.BlockSpec((1, tg, width), lambda b, i: (b, i, 0))

    def bwd(width):
        return pl.BlockSpec((1, tg, width), lambda b, i: (b, nb - 1 - i, 0))

    dec_f = pl.BlockSpec((1, ncb, 1, GLA_QKP), lambda b, i: (b, i, 0, 0))
    dec_b = pl.BlockSpec((1, ncb, 1, GLA_QKP), lambda b, i: (b, nb - 1 - i, 0, 0))
    st_spec = pl.BlockSpec((1, 2, GLA_V, GLA_QKP), lambda b, i: (b, 0, 0, 0))
    return pl.pallas_call(
        functools.partial(_gla_kernel, tg=tg),
        grid=(B, nb),
        in_specs=[fwd(GLA_QKP), fwd(GLA_QKP), fwd(GLA_QKP), fwd(GLA_V), dec_f,
                  bwd(GLA_QKP), bwd(GLA_QKP), bwd(GLA_QKP), bwd(GLA_V), dec_b,
                  st_spec, pl.BlockSpec(bd.shape, lambda b, i: (0, 0))],
        out_specs=[fwd(GLA_V), bwd(GLA_V), st_spec],
        out_shape=[jax.ShapeDtypeStruct((B, T, GLA_V), BF16),
                   jax.ShapeDtypeStruct((B, T, GLA_V), BF16),
                   jax.ShapeDtypeStruct((B, 2, GLA_V, GLA_QKP), F32)],
        scratch_shapes=[pltpu.VMEM((2, GLA_V, GLA_QKP), F32)],
        compiler_params=_params(2),
        name="gla",
    )(qdf, kif, kef, v, decf, qdb, kib, keb, v, decb, s0, bd)


def _swa_kernel(*refs, n_local, n_sub, seq_len):
    if n_local:
        q_ref, *loc_refs, ckv_ref, sink_ref, o_ref = refs
    else:
        q_ref, ckv_ref, sink_ref, o_ref = refs
        loc_refs = []
    QB = Q_BLOCK
    lane = lax.broadcasted_iota(jnp.int32, (QB, 128), 1)
    low = lane < HEAD_DIM
    n_loc = n_local * QB
    for sub in range(n_sub):
        _swa_block(pl.program_id(1) * n_sub + sub, sub, q_ref, loc_refs[sub:sub + n_local] + [ckv_ref],
                   sink_ref, o_ref, low, n_local=n_local, n_loc=n_loc, seq_len=seq_len)


def _swa_block(n, sub, q_ref, kv_parts, sink_ref, o_ref, low, *, n_local, n_loc, seq_len):
    QB = Q_BLOCK
    rows = slice(sub * QB, (sub + 1) * QB)
    kcat = jnp.concatenate([r[0, :, :SWA_KV] for r in kv_parts], axis=0)
    vcat = jnp.concatenate([r[0, :, SWA_KV:] for r in kv_parts], axis=0)
    if n_local:
        qi = lax.broadcasted_iota(jnp.int32, (QB, n_loc), 0)
        kj = lax.broadcasted_iota(jnp.int32, (QB, n_loc), 1)
        k_pos = (n - 1) * QB + kj
        dist = kj - QB - qi
        valid = (dist >= -WINDOW) & (dist <= WINDOW) & (k_pos >= 0) & (k_pos < seq_len)

    def scores(g):
        slabs = []
        for t in range(SWA_GROUP):
            qt = q_ref[0, rows, t * 128:(t + 1) * 128]
            slabs.append(jnp.where(low if g == 0 else ~low, qt, jnp.zeros_like(qt)))
        return lax.dot_general(jnp.concatenate(slabs, axis=0), kcat, _NT,
                               preferred_element_type=F32)

    def attend(g, s_grp):
        ps, dens = [], []
        for t in range(SWA_GROUP):
            sink = sink_ref[g * SWA_GROUP + t] * LOG2E
            sh = s_grp[t * QB:(t + 1) * QB]
            s_ctx = sh[:, n_loc:]
            m = jnp.maximum(jnp.max(s_ctx, axis=-1, keepdims=True), sink)
            if n_local:
                s_loc = jnp.where(valid, sh[:, :n_loc], NEG_INF)
                m = jnp.maximum(m, jnp.max(s_loc, axis=-1, keepdims=True))
                p_loc = jnp.exp2(s_loc - m)
            p_ctx = jnp.exp2(s_ctx - m)
            den = jnp.sum(p_ctx, axis=-1, keepdims=True) + jnp.exp2(sink - m)
            if n_local:
                den = den + jnp.sum(p_loc, axis=-1, keepdims=True)
                ps.append(jnp.concatenate([p_loc, p_ctx], axis=1).astype(BF16))
            else:
                ps.append(p_ctx.astype(BF16))
            dens.append(den)
        o_grp = _dot(jnp.concatenate(ps, axis=0), vcat)
        return [o_grp[t * QB:(t + 1) * QB] / dens[t] for t in range(SWA_GROUP)]

    s_groups = [scores(g) for g in range(SWA_KV_HEADS)]
    o_a = attend(0, s_groups[0])
    o_b = attend(1, s_groups[1])
    for t in range(SWA_GROUP):
        o_ref[0, rows, t * 128:(t + 1) * 128] = jnp.where(low, o_a[t], o_b[t]).astype(BF16)


def _swa(q, kv, ckv, sink, *, local):
    B, S, _ = q.shape
    L = ckv.shape[1]
    nq = S // Q_BLOCK
    n_sub = min(SWA_SUB, nq)
    qspec = pl.BlockSpec((1, n_sub * Q_BLOCK, SWA_Q), lambda b, n: (b, n, 0))
    cspec = pl.BlockSpec((1, L, 2 * SWA_KV), lambda b, n: (b, 0, 0))
    sspec = pl.BlockSpec(memory_space=pltpu.SMEM)
    if local:
        def blk(off):
            return pl.BlockSpec((1, Q_BLOCK, 2 * SWA_KV),
                                lambda b, n: (b, jnp.clip(n * n_sub + off, 0, nq - 1), 0))
        offs = range(-1, n_sub + 1)
        in_specs = [qspec] + [blk(off) for off in offs] + [cspec, sspec]
        args = (q,) + (kv,) * len(offs) + (ckv, sink)
    else:
        in_specs = [qspec, cspec, sspec]
        args = (q, ckv, sink)
    return pl.pallas_call(
        functools.partial(_swa_kernel, n_local=3 if local else 0, n_sub=n_sub, seq_len=S),
        grid=(B, nq // n_sub),
        in_specs=in_specs,
        out_specs=qspec,
        out_shape=jax.ShapeDtypeStruct((B, S, SWA_Q), BF16),
        compiler_params=_params(2),
        name="swa",
    )(*args)


def _out_kernel(x_ref, of_ref, ob_ref, gg_ref, swa_ref, pu_ref, pup_ref, pun_ref, icnt_ref, mod_ref,
                gng_ref, ones_ref, poolw_ref, pools_ref, wout_ref, g2_ref,
                x1_ref, h2_ref, ext_ref, cen_ref, hs_ref, *, tm):
    i = pl.program_id(0)
    nt = pl.num_programs(0)
    HP = POOL_HALF_MAX

    o = of_ref[0].astype(F32) + ob_ref[0].astype(F32)
    ss = _dot((o * o).astype(BF16), ones_ref[...])
    gg = gg_ref[0].astype(F32)
    gla = o * lax.rsqrt(ss * (1.0 / GLA_DV) + EPS) * gng_ref[...] * (gg * _sigmoid(gg))

    ext_ref[0:HP, :] = jnp.where(i > 0, pup_ref[0], 0.0)
    ext_ref[HP:HP + tm, :] = pu_ref[0]
    ext_ref[HP + tm:, :] = jnp.where(i < nt - 1, pun_ref[0], 0.0)
    upper_group = lax.broadcasted_iota(jnp.int32, (POOL_ROWS, 128), 1) >= POOL_GROUP
    for rb in range(tm // POOL_ROWS):
        r0 = HP + rb * POOL_ROWS
        rows = slice(rb * POOL_ROWS, (rb + 1) * POOL_ROWS)
        for tile in range(POOL_W // 128):
            ls = slice(tile * 128, (tile + 1) * 128)
            n_rows = POOL_ROWS + 2 * HP
            run = ext_ref[r0 - HP:r0 + POOL_ROWS + HP, ls]
            wins = []
            span = 1
            for w in POOL_WINDOWS:
                while span < w:
                    run = run + pltpu.roll(run, span, 0)
                    span *= 2
                if w in POOL_WINDOWS[2 * tile:2 * tile + 2]:
                    lead = w // 2 - 1
                    aligned = pltpu.roll(run, n_rows - lead, 0) if lead else run
                    wins.append(aligned[HP:HP + POOL_ROWS])
                if len(wins) == 2:
                    break
            win = jnp.where(upper_group, wins[1], wins[0])
            centred = win * icnt_ref[rows, ls] - ext_ref[r0:r0 + POOL_ROWS, ls]
            cen_ref[rows, ls] = centred.astype(BF16)
    pool = _dot(cen_ref[...], poolw_ref[...]) * pools_ref[...]

    cat = jnp.concatenate([gla.astype(BF16), swa_ref[0], pool.astype(BF16)], axis=1)
    y = _dot(cat, wout_ref[...])
    g1 = mod_ref[0, 2:3, :]
    sh2 = mod_ref[0, 3:4, :]
    sc2 = mod_ref[0, 4:5, :]
    x1 = x_ref[0] + g1 * y
    x1_ref[0] = x1
    h2 = _rms(x1) * (g2_ref[...] * (1.0 + sc2)) + sh2
    J = tm // TOKEN_RESIDUES
    for lt in range(D_MODEL // 128):
        ls = slice(lt * 128, (lt + 1) * 128)
        hs_ref[lt] = h2[:, ls]
        for k in range(TOKEN_RESIDUES):
            h2_ref[0, k * J:(k + 1) * J, ls] = hs_ref[lt, pl.ds(k, J, stride=TOKEN_RESIDUES), :].astype(BF16)


def _out_proj(x, o_f, o_b, gg, swa, pu, icnt, mod, mod_row, gng, ones96, poolw, pools, wout, g2, *, tm):
    B, T, D = x.shape
    nt = T // tm
    HP = POOL_HALF_MAX
    nh = T // HP

    def tok(width):
        return pl.BlockSpec((1, tm, width), lambda i, b: (b, i, 0))

    def const(arr):
        nd = arr.ndim
        return pl.BlockSpec(arr.shape, lambda i, b: (0,) * nd)

    prev = pl.BlockSpec((1, HP, POOL_W), lambda i, b: (b, jnp.maximum(i * (tm // HP) - 1, 0), 0))
    nxt = pl.BlockSpec((1, HP, POOL_W), lambda i, b: (b, jnp.minimum((i + 1) * (tm // HP), nh - 1), 0))
    return pl.pallas_call(
        functools.partial(_out_kernel, tm=tm),
        grid=(nt, B),
        in_specs=[tok(D), tok(GLA_V), tok(GLA_V), tok(GLA_V), tok(SWA_Q), tok(POOL_W), prev, nxt,
                  pl.BlockSpec((tm, POOL_W), lambda i, b: (i, 0)),
                  pl.BlockSpec((1, 6, D), lambda i, b: (mod_row(b), 0, 0)),
                  const(gng), const(ones96), const(poolw), const(pools), const(wout), const(g2)],
        out_specs=[tok(D), tok(D)],
        out_shape=[jax.ShapeDtypeStruct((B, T, D), F32), jax.ShapeDtypeStruct((B, T, D), BF16)],
        scratch_shapes=[pltpu.VMEM((tm + 2 * HP, POOL_W), F32), pltpu.VMEM((tm, POOL_W), BF16),
                        pltpu.VMEM((D // 128, tm, 128), F32)],
        compiler_params=_params(2),
        name="out_proj",
    )(x, o_f, o_b, gg, swa, pu, pu, pu, icnt, mod, gng, ones96, poolw, pools, wout, g2)


def _ffn_kernel(h_ref, hp_ref, hn_ref, x_ref, mod_ref, wup_ref, cwb_ref, wdn_ref,
                o_ref, hx_ref, u_ref, act_ref, nat_ref, *, tm):
    i = pl.program_id(0)
    nt = pl.num_programs(0)
    H = FFN_HALO
    N = N_FF_CHUNKS
    K = TOKEN_RESIDUES
    J = tm // K
    hx_ref[0:tm, :] = h_ref[0]
    before = jnp.where(i > 0, hp_ref[0].astype(F32)[H - 1:H], 0.0)
    after = jnp.where(i < nt - 1, hn_ref[0].astype(F32)[0:1], 0.0)
    halo_row = lax.broadcasted_iota(jnp.int32, (H, D_MODEL), 0)
    hx_ref[tm:, :] = jnp.where(halo_row == 0, before, jnp.where(halo_row == 1, after, 0.0)).astype(BF16)
    row8 = lax.broadcasted_iota(jnp.int32, (8, 128), 0)

    def up(fc, slot):
        for part in range(2):
            c0 = part * D_FF + fc * FF_CHUNK
            u_ref[slot, part] = _dot(hx_ref[...], wup_ref[:, c0:c0 + FF_CHUNK])

    def conv_act(fc, slot):
        for lt in range(FF_CHUNK // 128):
            ls = slice(lt * 128, (lt + 1) * 128)
            wts = [cwb_ref[part, fc, :, ls] for part in range(2)]
            for k in range(K):
                taps = []
                for part in range(2):
                    u = u_ref.at[slot, part]
                    if k > 0:
                        prev = u[(k - 1) * J:k * J, ls]
                    else:
                        rolled = pltpu.roll(u[(K - 1) * J:K * J, ls], 1, 0)
                        head = jnp.where(row8 == 0, u[tm:tm + 1, ls], rolled[0:8])
                        prev = jnp.concatenate([head, rolled[8:]], axis=0)
                    if k < K - 1:
                        nxt = u[(k + 1) * J:(k + 2) * J, ls]
                    else:
                        rolled = pltpu.roll(u[0:J, ls], J - 1, 0)
                        tail = jnp.where(row8 == 7, u[tm + 1:tm + 2, ls], rolled[J - 8:J])
                        nxt = jnp.concatenate([rolled[:J - 8], tail], axis=0)
                    w = wts[part]
                    taps.append(prev * w[0:1] + u[k * J:(k + 1) * J, ls] * w[1:2] + nxt * w[2:3] + w[3:4])
                a, g = taps
                act_ref[fc, k * J:(k + 1) * J, ls] = ((g * _sigmoid(g)) * a).astype(BF16)

    up(0, 0)
    for s in range(1, N + 1):
        if s < N:
            up(s, s % 2)
        conv_act(s - 1, (s - 1) % 2)
    act = jnp.concatenate([act_ref[c] for c in range(N)], axis=1)
    y = mod_ref[0, 5:6, :] * _dot(act, wdn_ref[...])
    for lt in range(D_MODEL // 128):
        ls = slice(lt * 128, (lt + 1) * 128)
        for k in range(K):
            nat_ref[lt, pl.ds(k, J, stride=K), :] = y[k * J:(k + 1) * J, ls]
        o_ref[0, :, ls] = x_ref[0, :, ls] + nat_ref[lt]


def _ffn(h2, x1, mod, mod_row, wup, cwb, wdn, *, tm):
    B, T, D = x1.shape
    nt = T // tm
    H = FFN_HALO
    nh = T // H

    def tok(width):
        return pl.BlockSpec((1, tm, width), lambda i, b: (b, i, 0))

    def const(arr):
        nd = arr.ndim
        return pl.BlockSpec(arr.shape, lambda i, b: (0,) * nd)

    prev = pl.BlockSpec((1, H, D), lambda i, b: (b, jnp.maximum(i * (tm // H) - 1, 0), 0))
    nxt = pl.BlockSpec((1, H, D), lambda i, b: (b, jnp.minimum((i + 1) * (tm // H), nh - 1), 0))
    return pl.pallas_call(
        functools.partial(_ffn_kernel, tm=tm),
        grid=(nt, B),
        in_specs=[tok(D), prev, nxt, tok(D),
                  pl.BlockSpec((1, 6, D), lambda i, b: (mod_row(b), 0, 0)),
                  const(wup), const(cwb), const(wdn)],
        out_specs=tok(D),
        out_shape=jax.ShapeDtypeStruct((B, T, D), F32),
        scratch_shapes=[pltpu.VMEM((tm + H, D), BF16),
                        pltpu.VMEM((2, 2, tm + H, FF_CHUNK), F32),
                        pltpu.VMEM((N_FF_CHUNKS, tm, FF_CHUNK), BF16),
                        pltpu.VMEM((D // 128, tm, 128), F32)],
        compiler_params=_params(2),
        name="ffn",
    )(h2, h2, h2, x1, mod, wup, cwb, wdn)


def _pad_heads(a, n_heads, width, padded):
    a = a.reshape(a.shape[:-1] + (n_heads, width))
    a = jnp.pad(a, [(0, 0)] * (a.ndim - 1) + [(0, padded - width)])
    return a.reshape(a.shape[:-2] + (n_heads * padded,))


def _permute_heads(a, axis):
    shape = a.shape
    axis = axis % a.ndim
    a = a.reshape(shape[:axis] + (SWA_HEADS, HEAD_DIM) + shape[axis + 1:])
    a = jnp.take(a, jnp.array(SWA_HEAD_ORDER), axis=axis)
    return a.reshape(shape)


def _block_diag_ones(n, block):
    idx = np.arange(n) // block
    return jnp.asarray(idx[:, None] == idx[None, :], BF16)


def _pack_params(w_in, gla_w_dec, gla_b_dec, q_norm_g, k_norm_g, pool_w, w_out, conv_w, conv_b):
    depth = w_in.shape[0]
    gq, gk, gv, gg, zf, zb, aq, ak, av, pu = jnp.split(
        w_in, np.cumsum([192, 192, 384, 384, 16, 16, 384, 128, 128])[:].tolist(), axis=-1)
    z = jnp.pad(jnp.concatenate([zf, zb], axis=-1), ((0, 0), (0, 0), (0, Z_PAD - 2 * GLA_RANK)))
    w = jnp.concatenate([_pad_heads(gq, GLA_HEADS, GLA_DK, GLA_DKP),
                         _pad_heads(gk, GLA_HEADS, GLA_DK, GLA_DKP),
                         gv, gg, z, av, _permute_heads(aq, -1), ak, pu], axis=-1).astype(BF16)
    dec = _pad_heads(gla_w_dec, GLA_HEADS, GLA_DK, GLA_DKP)
    none = jnp.zeros_like(dec[:, 0])
    wdec = jnp.concatenate([jnp.concatenate([dec[:, 0], none], axis=-1),
                            jnp.concatenate([none, dec[:, 1]], axis=-1)], axis=-2)
    wdec = jnp.pad(wdec, ((0, 0), (0, Z_PAD - 2 * GLA_RANK), (0, 0))).astype(BF16)
    bdec = _pad_heads(gla_b_dec, GLA_HEADS, GLA_DK, GLA_DKP).reshape(depth, 1, 2 * GLA_QKP)
    qkg = jnp.concatenate([jnp.tile(q_norm_g, (1, SWA_HEADS)) * (HEAD_DIM ** -0.5 * LOG2E),
                           jnp.tile(k_norm_g, (1, SWA_KV_HEADS))], axis=-1)[:, None, :]
    n_groups = len(POOL_WINDOWS)
    poolw = jnp.concatenate(
        [jnp.pad(pool_w[:, g], ((0, 0), (0, 0), (g * POOL_GROUP, (n_groups - 1 - g) * POOL_GROUP)))
         for g in range(n_groups)], axis=-2).astype(BF16)
    wout = jnp.concatenate([w_out[:, :GLA_V], _permute_heads(w_out[:, GLA_V:GLA_V + SWA_Q], -2),
                            w_out[:, GLA_V + SWA_Q:]], axis=-2).astype(BF16)
    cwb = jnp.concatenate([conv_w, conv_b[:, None, :]], axis=-2).reshape(depth, 4, 2, N_FF_CHUNKS, FF_CHUNK)
    return dict(w=w, wdec=wdec, bdec=bdec, qkg=qkg, poolw=poolw, wout=wout,
                cwb=jnp.transpose(cwb, (0, 2, 3, 1, 4)))


def _rope_tables(n_tokens):
    rows_n = n_tokens // GRID_W
    rows = jnp.repeat(jnp.arange(rows_n), GRID_W).astype(F32)
    cols = jnp.tile(jnp.arange(GRID_W), rows_n).astype(F32)
    nf = HEAD_DIM // 4
    inv = ROPE_BASE ** (-jnp.arange(nf, dtype=F32) / nf)
    ar = rows[:, None] * inv
    ac = cols[:, None] * inv
    cos = jnp.concatenate([jnp.cos(ar), jnp.cos(ar), jnp.cos(ac), jnp.cos(ac)], axis=-1)
    sin = jnp.concatenate([-jnp.sin(ar), jnp.sin(ar), -jnp.sin(ac), jnp.sin(ac)], axis=-1)
    return jnp.tile(cos, (1, 128 // HEAD_DIM)), jnp.tile(sin, (1, 128 // HEAD_DIM))


def _pool_inv_counts(n_tokens):
    t = jnp.arange(n_tokens)[:, None]
    half = jnp.repeat(jnp.asarray(POOL_WINDOWS) // 2, POOL_GROUP)[None, :]
    cnt = jnp.minimum(t + half, n_tokens) - jnp.maximum(t - half, 0)
    return 1.0 / cnt.astype(F32)


def _tile_sizes(T):
    return dict(tm=min(T, 512), tg=min(T, 512), tf=min(T, 512))


def kernel(x, c, ctx, c_ctx, w_ada, b_ada, norm1_g, w_in, gla_w_dec, gla_b_dec, gla_norm_g, q_norm_g,
           k_norm_g, sink_logit, pool_w, pool_scale, w_out, norm2_g, w_up, conv_w, conv_b, w_down):
    B, S, D = x.shape
    L = ctx.shape[1]
    c_rows = jnp.zeros((8, D), F32).at[:B].set(c).at[B].set(c_ctx)
    mod_all = _modulation(c_rows, w_ada, b_ada).reshape(DEPTH, 8, 6, D)
    cos, sin = _rope_tables(S)
    chunk_ones = np.ones((GLA_CHUNK, GLA_CHUNK), np.float32)
    tri = jnp.asarray(np.stack([np.tril(chunk_ones), np.triu(chunk_ones)]), BF16)
    ones64 = _block_diag_ones(SWA_Q + SWA_KV, HEAD_DIM) * jnp.asarray(1.0 / HEAD_DIM, BF16)
    ones96 = _block_diag_ones(GLA_V, GLA_DV)
    icnt_lat, icnt_ctx = _pool_inv_counts(S), _pool_inv_counts(L)
    bd = jnp.asarray((np.arange(GLA_V) // GLA_DV)[:, None] == (np.arange(GLA_QKP) // GLA_DKP)[None, :], F32)
    lat, cx = _tile_sizes(S), _tile_sizes(L)
    lat_row = lambda b: b
    ctx_row = lambda b: B
    zero_state = jnp.zeros((B, 2, GLA_V, GLA_QKP), F32)

    packed = _pack_params(w_in, gla_w_dec, gla_b_dec, q_norm_g, k_norm_g, pool_w, w_out, conv_w, conv_b)
    w_up_bf, w_down_bf = w_up.astype(BF16), w_down.astype(BF16)

    for l in range(DEPTH):
        p = {name: arr[l] for name, arr in packed.items()}
        mod = mod_all[l]
        g1 = norm1_g[l][None, :]
        g2 = norm2_g[l][None, :]
        gng = jnp.tile(gla_norm_g[l], GLA_HEADS)[None, :]
        pools = pool_scale[l][None, :]
        wup = w_up_bf[l]
        wdn = w_down_bf[l]
        shared = (g1, p["w"], p["wdec"], p["bdec"], tri, ones64, p["qkg"])
        update_ctx = l < DEPTH - 1

        (cqdf, ckif, ckef, cqdb, ckib, ckeb, cdecf, cdecb, cgv, cgg, csq, cskv, cpu) = _in_proj(
            ctx, mod, ctx_row, *shared, cos[:L], sin[:L], tm=cx["tm"], rope=False)
        co_f, co_b, st = _gla(cqdf, ckif, ckef, cqdb, ckib, ckeb, cdecf, cdecb, cgv, zero_state, bd,
                              tg=cx["tg"])

        (qdf, kif, kef, qdb, kib, keb, decf, decb, gv, gg, sq, skv, pu) = _in_proj(
            x, mod, lat_row, *shared, cos, sin, tm=lat["tm"], rope=True)
        o_f, o_b, _ = _gla(qdf, kif, kef, qdb, kib, keb, decf, decb, gv, st, bd, tg=lat["tg"])
        swa = _swa(sq, skv, cskv, sink_logit[l], local=True)
        x1, h2 = _out_proj(x, o_f, o_b, gg, swa, pu, icnt_lat, mod, lat_row, gng, ones96, p["poolw"],
                           pools, p["wout"], g2, tm=lat["tm"])
        x = _ffn(h2, x1, mod, lat_row, wup, p["cwb"], wdn, tm=lat["tf"])

        if update_ctx:
            cswa = _swa(csq, None, cskv, sink_logit[l], local=False)
            c1, ch2 = _out_proj(ctx, co_f, co_b, cgg, cswa, cpu, icnt_ctx, mod, ctx_row, gng, ones96,
                                p["poolw"], pools, p["wout"], g2, tm=cx["tm"])
            ctx = _ffn(ch2, c1, mod, ctx_row, wup, p["cwb"], wdn, tm=cx["tf"])
    return x
```

```python
import functools

import jax
import jax.numpy as jnp
import numpy as np
from jax import lax
from jax.experimental import pallas as pl
from jax.experimental.pallas import tpu as pltpu

F32 = jnp.float32
BF16 = jnp.bfloat16

D_MODEL = 1024
DEPTH = 2
GRID_W = 64
GLA_HEADS = 4
GLA_DK = 48
GLA_DKP = 64
GLA_DV = 96
GLA_RANK = 16
GLA_TAU = 16.0
GLA_CHUNK = 64
SWA_HEADS = 6
SWA_KV_HEADS = 2
SWA_GROUP = SWA_HEADS // SWA_KV_HEADS
HEAD_DIM = 64
WINDOW = 128
Q_BLOCK = 128
SWA_SUB = 8
ROPE_BASE = 10000.0
POOL_WINDOWS = (2, 4, 8, 16)
POOL_GROUP = 64
POOL_HALF_MAX = max(POOL_WINDOWS) // 2
POOL_ROWS = 128
D_FF = 2816
EPS = 1e-6
NEG_INF = -1e30
LOG2E = 1.4426950408889634

GLA_QKP = GLA_HEADS * GLA_DKP
GLA_V = GLA_HEADS * GLA_DV
SWA_Q = SWA_HEADS * HEAD_DIM
SWA_KV = SWA_KV_HEADS * HEAD_DIM
POOL_W = len(POOL_WINDOWS) * POOL_GROUP
Z_PAD = 128

OFF_Q = 0
OFF_K = OFF_Q + GLA_QKP
OFF_GV = OFF_K + GLA_QKP
OFF_GG = OFF_GV + GLA_V
OFF_Z = OFF_GG + GLA_V
OFF_SV = OFF_Z + Z_PAD
OFF_SQ = OFF_SV + SWA_KV
OFF_SK = OFF_SQ + SWA_Q
OFF_PU = OFF_SK + SWA_KV
IN_WP = OFF_PU + POOL_W

SWA_HEAD_ORDER = tuple(h for t in range(SWA_GROUP) for h in (t, t + SWA_GROUP))

FF_CHUNK = 256
N_FF_CHUNKS = D_FF // FF_CHUNK
FFN_HALO = 16
TOKEN_RESIDUES = 8
V7X_VMEM_LIMIT = 56 * 1024 * 1024

_NT = (((1,), (1,)), ((), ()))
_TN = (((0,), (0,)), ((), ()))


def _dot(a, b):
    return jnp.dot(a, b, preferred_element_type=F32)


def _params(n_grid, flags=None):
    return pltpu.CompilerParams(dimension_semantics=("arbitrary",) * n_grid,
                                vmem_limit_bytes=V7X_VMEM_LIMIT, flags=flags)


def _sigmoid(x):
    return 1.0 / (1.0 + jnp.exp2(x * -LOG2E))


def _log_sigmoid(x):
    return jnp.minimum(x, 0.0) - jnp.log(1.0 + jnp.exp2(jnp.abs(x) * -LOG2E))


def _rms(x):
    return x * lax.rsqrt(jnp.mean(x * x, axis=-1, keepdims=True) + EPS)


def _mod_kernel(c_ref, w_ref, b_ref, o_ref):
    c = c_ref[...]
    act = (c * _sigmoid(c)).astype(BF16)
    o_ref[0] = _dot(act, w_ref[0].astype(BF16)) + b_ref[0]


def _modulation(c_rows, w_ada, b_ada):
    tn = 1536
    n = w_ada.shape[-1]
    return pl.pallas_call(
        _mod_kernel,
        grid=(DEPTH, n // tn),
        in_specs=[pl.BlockSpec((8, D_MODEL), lambda l, j: (0, 0)),
                  pl.BlockSpec((1, D_MODEL, tn), lambda l, j: (l, 0, j)),
                  pl.BlockSpec((1, 1, tn), lambda l, j: (l, 0, j))],
        out_specs=pl.BlockSpec((1, 8, tn), lambda l, j: (l, 0, j)),
        out_shape=jax.ShapeDtypeStruct((DEPTH, 8, n), F32),
        compiler_params=_params(2),
        name="adaln_mod",
    )(c_rows, w_ada, b_ada.reshape(DEPTH, 1, n))


def _in_proj_kernel(x_ref, mod_ref, g1_ref, w_ref, wdec_ref, bdec_ref, tri_ref, ones_ref, qkg_ref,
                    cos_ref, sin_ref,
                    qdf_ref, kif_ref, kef_ref, qdb_ref, kib_ref, keb_ref, decf_ref, decb_ref,
                    gv_ref, gg_ref, sq_ref, skv_ref, pu_ref, *, tm, rope):
    x = x_ref[0]
    sh1 = mod_ref[0, 0:1, :]
    sc1 = mod_ref[0, 1:2, :]
    h = _rms(x) * (g1_ref[...] * (1.0 + sc1)) + sh1
    hb = h.astype(BF16)

    qk = _dot(hb, w_ref[:, OFF_Q:OFF_GV])
    q = qk[:, :GLA_QKP] * (GLA_DK ** -0.5)
    k = qk[:, GLA_QKP:]
    zsv = _dot(hb, w_ref[:, OFF_Z:OFF_SQ])
    z = zsv[:, :Z_PAD]
    skv_ref[0, :, SWA_KV:] = zsv[:, Z_PAD:].astype(BF16)
    la = _log_sigmoid(_dot(z.astype(BF16), wdec_ref[...]) + bdec_ref[...]) * (LOG2E / GLA_TAU)
    la_hi = la.astype(BF16)
    la_lo = (la - la_hi.astype(F32)).astype(BF16)
    lower = tri_ref[0]
    upper = tri_ref[1]
    C = GLA_CHUNK
    for c in range(tm // C):
        r = slice(c * C, (c + 1) * C)
        bcf = _dot(lower, la_hi[r, :GLA_QKP]) + _dot(lower, la_lo[r, :GLA_QKP])
        bcb = _dot(upper, la_hi[r, GLA_QKP:]) + _dot(upper, la_lo[r, GLA_QKP:])
        blf = bcf[C - 1:C, :]
        blb = bcb[0:1, :]
        qc = q[r]
        kc = k[r]
        qdf_ref[0, r, :] = (qc * jnp.exp2(bcf)).astype(BF16)
        kif_ref[0, r, :] = (kc * jnp.exp2(-bcf)).astype(BF16)
        kef_ref[0, r, :] = (kc * jnp.exp2(blf - bcf)).astype(BF16)
        qdb_ref[0, r, :] = (qc * jnp.exp2(bcb)).astype(BF16)
        kib_ref[0, r, :] = (kc * jnp.exp2(-bcb)).astype(BF16)
        keb_ref[0, r, :] = (kc * jnp.exp2(blb - bcb)).astype(BF16)
        decf_ref[0, c] = jnp.exp2(blf)
        decb_ref[0, c] = jnp.exp2(blb)

    vg = _dot(hb, w_ref[:, OFF_GV:OFF_Z])
    gv_ref[0] = vg[:, :GLA_V].astype(BF16)
    gg_ref[0] = vg[:, GLA_V:].astype(BF16)

    sqk = _dot(hb, w_ref[:, OFF_SQ:OFF_PU])
    ss = _dot((sqk * sqk).astype(BF16), ones_ref[...])
    sqk = sqk * lax.rsqrt(ss + EPS) * qkg_ref[...]
    tiles = []
    for t in range((SWA_Q + SWA_KV) // 128):
        xt = sqk[:, t * 128:(t + 1) * 128]
        if rope:
            lane = lax.broadcasted_iota(jnp.int32, xt.shape, 1)
            first = (lane % 32) < 16
            partner = jnp.where(first, pltpu.roll(xt, 128 - 16, 1), pltpu.roll(xt, 16, 1))
            xt = xt * cos_ref[...] + partner * sin_ref[...]
        tiles.append(xt.astype(BF16))
    for t in range(SWA_Q // 128):
        sq_ref[0, :, t * 128:(t + 1) * 128] = tiles[t]
    skv_ref[0, :, :SWA_KV] = tiles[-1]

    pu_ref[0] = _dot(hb, w_ref[:, OFF_PU:IN_WP])


def _in_proj(x, mod, mod_row, g1, w, wdec, bdec, tri, ones64, qkg, cos, sin, *, tm, rope):
    B, T, D = x.shape
    nt = T // tm
    nc = T // GLA_CHUNK

    def tok(width, dtype):
        return (jax.ShapeDtypeStruct((B, T, width), dtype),
                pl.BlockSpec((1, tm, width), lambda i, b: (b, i, 0)))

    def const(arr):
        nd = arr.ndim
        return pl.BlockSpec(arr.shape, lambda i, b: (0,) * nd)

    dec = (jax.ShapeDtypeStruct((B, nc, 1, GLA_QKP), F32),
           pl.BlockSpec((1, tm // GLA_CHUNK, 1, GLA_QKP), lambda i, b: (b, i, 0, 0)))
    outs = [tok(GLA_QKP, BF16)] * 6 + [dec, dec] + [tok(GLA_V, BF16), tok(GLA_V, BF16),
                                                   tok(SWA_Q, BF16), tok(2 * SWA_KV, BF16),
                                                   tok(POOL_W, F32)]
    return pl.pallas_call(
        functools.partial(_in_proj_kernel, tm=tm, rope=rope),
        grid=(nt, B),
        in_specs=[pl.BlockSpec((1, tm, D), lambda i, b: (b, i, 0)),
                  pl.BlockSpec((1, 6, D), lambda i, b: (mod_row(b), 0, 0)),
                  const(g1), const(w), const(wdec), const(bdec), const(tri), const(ones64), const(qkg),
                  pl.BlockSpec((tm, 128), lambda i, b: (i, 0)),
                  pl.BlockSpec((tm, 128), lambda i, b: (i, 0))],
        out_specs=[o[1] for o in outs],
        out_shape=[o[0] for o in outs],
        compiler_params=_params(2),
        name="in_proj",
    )(x, mod, g1, w, wdec, bdec, tri, ones64, qkg, cos, sin)


def _gla_kernel(qdf_ref, kif_ref, kef_ref, vf_ref, decf_ref, qdb_ref, kib_ref, keb_ref, vb_ref, decb_ref,
                s0_ref, bd_ref, of_ref, ob_ref, sfin_ref, st_ref, *, tg):
    i = pl.program_id(1)
    nb = pl.num_programs(1)
    C = GLA_CHUNK
    ncb = tg // C

    @pl.when(i == 0)
    def _():
        st_ref[...] = s0_ref[0]

    HC = GLA_HEADS * C
    row_head = lax.broadcasted_iota(jnp.int32, (HC, GLA_QKP), 0) // C
    k_mask = row_head == lax.broadcasted_iota(jnp.int32, (HC, GLA_QKP), 1) // GLA_DKP
    vlane = lax.broadcasted_iota(jnp.int32, (HC, GLA_V), 1)
    vhead = sum((vlane >= h * GLA_DV).astype(jnp.int32) for h in range(1, GLA_HEADS))
    v_mask = lax.broadcasted_iota(jnp.int32, (HC, GLA_V), 0) // C == vhead
    row_i = lax.broadcasted_iota(jnp.int32, (C, HC), 0)
    col_j = lax.broadcasted_iota(jnp.int32, (C, HC), 1) % C
    bd = bd_ref[...]

    def chunk(d, qd_ref, ki_ref, ke_ref, v_ref, dec_ref, o_ref, c):
        r = slice(c * C, (c + 1) * C)
        qd = qd_ref[0, r, :]
        ki = ki_ref[0, r, :]
        ke = ke_ref[0, r, :]
        vv = v_ref[0, r, :]
        k_bd = jnp.where(k_mask, jnp.concatenate([ki] * GLA_HEADS, axis=0), jnp.zeros((), BF16))
        v_bd = jnp.where(v_mask, jnp.concatenate([vv] * GLA_HEADS, axis=0), jnp.zeros((), BF16))
        st = st_ref[d]
        rhs = jnp.concatenate([k_bd, st.astype(BF16)], axis=0)
        res = lax.dot_general(qd, rhs, _NT, preferred_element_type=F32)
        keep = (col_j <= row_i) if d == 0 else (col_j >= row_i)
        p = jnp.where(keep, res[:, :HC], 0.0).astype(BF16)
        o = res[:, HC:] + _dot(p, v_bd)
        o_ref[0, r, :] = o.astype(BF16)
        upd = lax.dot_general(vv, ke, _TN, preferred_element_type=F32)
        st_ref[d] = st * dec_ref[0, c] + upd * bd

    for c in range(ncb):
        chunk(0, qdf_ref, kif_ref, kef_ref, vf_ref, decf_ref, of_ref, c)
        chunk(1, qdb_ref, kib_ref, keb_ref, vb_ref, decb_ref, ob_ref, ncb - 1 - c)

    @pl.when(i == nb - 1)
    def _():
        sfin_ref[0] = st_ref[...]


def _gla(qdf, kif, kef, qdb, kib, keb, decf, decb, v, s0, bd, *, tg):
    B, T, _ = v.shape
    nb = T // tg
    ncb = tg // GLA_CHUNK

    def fwd(width):
        return pl.BlockSpec((1, tg, width), lambda b, i: (b, i, 0))

    def bwd(width):
        return pl.BlockSpec((1, tg, width), lambda b, i: (b, nb - 1 - i, 0))

    dec_f = pl.BlockSpec((1, ncb, 1, GLA_QKP), lambda b, i: (b, i, 0, 0))
    dec_b = pl.BlockSpec((1, ncb, 1, GLA_QKP), lambda b, i: (b, nb - 1 - i, 0, 0))
    st_spec = pl.BlockSpec((1, 2, GLA_V, GLA_QKP), lambda b, i: (b, 0, 0, 0))
    return pl.pallas_call(
        functools.partial(_gla_kernel, tg=tg),
        grid=(B, nb),
        in_specs=[fwd(GLA_QKP), fwd(GLA_QKP), fwd(GLA_QKP), fwd(GLA_V), dec_f,
                  bwd(GLA_QKP), bwd(GLA_QKP), bwd(GLA_QKP), bwd(GLA_V), dec_b,
                  st_spec, pl.BlockSpec(bd.shape, lambda b, i: (0, 0))],
        out_specs=[fwd(GLA_V), bwd(GLA_V), st_spec],
        out_shape=[jax.ShapeDtypeStruct((B, T, GLA_V), BF16),
                   jax.ShapeDtypeStruct((B, T, GLA_V), BF16),
                   jax.ShapeDtypeStruct((B, 2, GLA_V, GLA_QKP), F32)],
        scratch_shapes=[pltpu.VMEM((2, GLA_V, GLA_QKP), F32)],
        compiler_params=_params(2),
        name="gla",
    )(qdf, kif, kef, v, decf, qdb, kib, keb, v, decb, s0, bd)


def _swa_kernel(*refs, n_local, n_sub, seq_len):
    if n_local:
        q_ref, *loc_refs, ckv_ref, sink_ref, o_ref = refs
    else:
        q_ref, ckv_ref, sink_ref, o_ref = refs
        loc_refs = []
    QB = Q_BLOCK
    lane = lax.broadcasted_iota(jnp.int32, (QB, 128), 1)
    low = lane < HEAD_DIM
    n_loc = n_local * QB
    for sub in range(n_sub):
        _swa_block(pl.program_id(1) * n_sub + sub, sub, q_ref, loc_refs[sub:sub + n_local] + [ckv_ref],
                   sink_ref, o_ref, low, n_local=n_local, n_loc=n_loc, seq_len=seq_len)


def _swa_block(n, sub, q_ref, kv_parts, sink_ref, o_ref, low, *, n_local, n_loc, seq_len):
    QB = Q_BLOCK
    rows = slice(sub * QB, (sub + 1) * QB)
    kcat = jnp.concatenate([r[0, :, :SWA_KV] for r in kv_parts], axis=0)
    vcat = jnp.concatenate([r[0, :, SWA_KV:] for r in kv_parts], axis=0)
    if n_local:
        qi = lax.broadcasted_iota(jnp.int32, (QB, n_loc), 0)
        kj = lax.broadcasted_iota(jnp.int32, (QB, n_loc), 1)
        k_pos = (n - 1) * QB + kj
        dist = kj - QB - qi
        valid = (dist >= -WINDOW) & (dist <= WINDOW) & (k_pos >= 0) & (k_pos < seq_len)

    def scores(g):
        slabs = []
        for t in range(SWA_GROUP):
            qt = q_ref[0, rows, t * 128:(t + 1) * 128]
            slabs.append(jnp.where(low if g == 0 else ~low, qt, jnp.zeros_like(qt)))
        return lax.dot_general(jnp.concatenate(slabs, axis=0), kcat, _NT,
                               preferred_element_type=F32)

    def attend(g, s_grp):
        ps, dens = [], []
        for t in range(SWA_GROUP):
            sink = sink_ref[g * SWA_GROUP + t] * LOG2E
            sh = s_grp[t * QB:(t + 1) * QB]
            s_ctx = sh[:, n_loc:]
            m = jnp.maximum(jnp.max(s_ctx, axis=-1, keepdims=True), sink)
            if n_local:
                s_loc = jnp.where(valid, sh[:, :n_loc], NEG_INF)
                m = jnp.maximum(m, jnp.max(s_loc, axis=-1, keepdims=True))
                p_loc = jnp.exp2(s_loc - m)
            p_ctx = jnp.exp2(s_ctx - m)
            den = jnp.sum(p_ctx, axis=-1, keepdims=True) + jnp.exp2(sink - m)
            if n_local:
                den = den + jnp.sum(p_loc, axis=-1, keepdims=True)
                ps.append(jnp.concatenate([p_loc, p_ctx], axis=1).astype(BF16))
            else:
                ps.append(p_ctx.astype(BF16))
            dens.append(den)
        o_grp = _dot(jnp.concatenate(ps, axis=0), vcat)
        return [o_grp[t * QB:(t + 1) * QB] / dens[t] for t in range(SWA_GROUP)]

    s_groups = [scores(g) for g in range(SWA_KV_HEADS)]
    o_a = attend(0, s_groups[0])
    o_b = attend(1, s_groups[1])
    for t in range(SWA_GROUP):
        o_ref[0, rows, t * 128:(t + 1) * 128] = jnp.where(low, o_a[t], o_b[t]).astype(BF16)


def _swa(q, kv, ckv, sink, *, local):
    B, S, _ = q.shape
    L = ckv.shape[1]
    nq = S // Q_BLOCK
    n_sub = min(SWA_SUB, nq)
    qspec = pl.BlockSpec((1, n_sub * Q_BLOCK, SWA_Q), lambda b, n: (b, n, 0))
    cspec = pl.BlockSpec((1, L, 2 * SWA_KV), lambda b, n: (b, 0, 0))
    sspec = pl.BlockSpec(memory_space=pltpu.SMEM)
    if local:
        def blk(off):
            return pl.BlockSpec((1, Q_BLOCK, 2 * SWA_KV),
                                lambda b, n: (b, jnp.clip(n * n_sub + off, 0, nq - 1), 0))
        offs = range(-1, n_sub + 1)
        in_specs = [qspec] + [blk(off) for off in offs] + [cspec, sspec]
        args = (q,) + (kv,) * len(offs) + (ckv, sink)
    else:
        in_specs = [qspec, cspec, sspec]
        args = (q, ckv, sink)
    return pl.pallas_call(
        functools.partial(_swa_kernel, n_local=3 if local else 0, n_sub=n_sub, seq_len=S),
        grid=(B, nq // n_sub),
        in_specs=in_specs,
        out_specs=qspec,
        out_shape=jax.ShapeDtypeStruct((B, S, SWA_Q), BF16),
        compiler_params=_params(2),
        name="swa",
    )(*args)


def _out_kernel(x_ref, of_ref, ob_ref, gg_ref, swa_ref, pu_ref, pup_ref, pun_ref, icnt_ref, mod_ref,
                gng_ref, ones_ref, poolw_ref, pools_ref, wout_ref, g2_ref,
                x1_ref, h2_ref, ext_ref, cen_ref, hs_ref, *, tm):
    i = pl.program_id(0)
    nt = pl.num_programs(0)
    HP = POOL_HALF_MAX

    o = of_ref[0].astype(F32) + ob_ref[0].astype(F32)
    ss = _dot((o * o).astype(BF16), ones_ref[...])
    gg = gg_ref[0].astype(F32)
    gla = o * lax.rsqrt(ss * (1.0 / GLA_DV) + EPS) * gng_ref[...] * (gg * _sigmoid(gg))

    ext_ref[0:HP, :] = jnp.where(i > 0, pup_ref[0], 0.0)
    ext_ref[HP:HP + tm, :] = pu_ref[0]
    ext_ref[HP + tm:, :] = jnp.where(i < nt - 1, pun_ref[0], 0.0)
    upper_group = lax.broadcasted_iota(jnp.int32, (POOL_ROWS, 128), 1) >= POOL_GROUP
    for rb in range(tm // POOL_ROWS):
        r0 = HP + rb * POOL_ROWS
        rows = slice(rb * POOL_ROWS, (rb + 1) * POOL_ROWS)
        for tile in range(POOL_W // 128):
            ls = slice(tile * 128, (tile + 1) * 128)
            n_rows = POOL_ROWS + 2 * HP
            run = ext_ref[r0 - HP:r0 + POOL_ROWS + HP, ls]
            wins = []
            span = 1
            for w in POOL_WINDOWS:
                while span < w:
                    run = run + pltpu.roll(run, span, 0)
                    span *= 2
                if w in POOL_WINDOWS[2 * tile:2 * tile + 2]:
                    lead = w // 2 - 1
                    aligned = pltpu.roll(run, n_rows - lead, 0) if lead else run
                    wins.append(aligned[HP:HP + POOL_ROWS])
                if len(wins) == 2:
                    break
            win = jnp.where(upper_group, wins[1], wins[0])
            centred = win * icnt_ref[rows, ls] - ext_ref[r0:r0 + POOL_ROWS, ls]
            cen_ref[rows, ls] = centred.astype(BF16)
    pool = _dot(cen_ref[...], poolw_ref[...]) * pools_ref[...]

    cat = jnp.concatenate([gla.astype(BF16), swa_ref[0], pool.astype(BF16)], axis=1)
    y = _dot(cat, wout_ref[...])
    g1 = mod_ref[0, 2:3, :]
    sh2 = mod_ref[0, 3:4, :]
    sc2 = mod_ref[0, 4:5, :]
    x1 = x_ref[0] + g1 * y
    x1_ref[0] = x1
    h2 = _rms(x1) * (g2_ref[...] * (1.0 + sc2)) + sh2
    J = tm // TOKEN_RESIDUES
    for lt in range(D_MODEL // 128):
        ls = slice(lt * 128, (lt + 1) * 128)
        hs_ref[lt] = h2[:, ls]
        for k in range(TOKEN_RESIDUES):
            h2_ref[0, k * J:(k + 1) * J, ls] = hs_ref[lt, pl.ds(k, J, stride=TOKEN_RESIDUES), :].astype(BF16)


def _out_proj(x, o_f, o_b, gg, swa, pu, icnt, mod, mod_row, gng, ones96, poolw, pools, wout, g2, *, tm):
    B, T, D = x.shape
    nt = T // tm
    HP = POOL_HALF_MAX
    nh = T // HP

    def tok(width):
        return pl.BlockSpec((1, tm, width), lambda i, b: (b, i, 0))

    def const(arr):
        nd = arr.ndim
        return pl.BlockSpec(arr.shape, lambda i, b: (0,) * nd)

    prev = pl.BlockSpec((1, HP, POOL_W), lambda i, b: (b, jnp.maximum(i * (tm // HP) - 1, 0), 0))
    nxt = pl.BlockSpec((1, HP, POOL_W), lambda i, b: (b, jnp.minimum((i + 1) * (tm // HP), nh - 1), 0))
    return pl.pallas_call(
        functools.partial(_out_kernel, tm=tm),
        grid=(nt, B),
        in_specs=[tok(D), tok(GLA_V), tok(GLA_V), tok(GLA_V), tok(SWA_Q), tok(POOL_W), prev, nxt,
                  pl.BlockSpec((tm, POOL_W), lambda i, b: (i, 0)),
                  pl.BlockSpec((1, 6, D), lambda i, b: (mod_row(b), 0, 0)),
                  const(gng), const(ones96), const(poolw), const(pools), const(wout), const(g2)],
        out_specs=[tok(D), tok(D)],
        out_shape=[jax.ShapeDtypeStruct((B, T, D), F32), jax.ShapeDtypeStruct((B, T, D), BF16)],
        scratch_shapes=[pltpu.VMEM((tm + 2 * HP, POOL_W), F32), pltpu.VMEM((tm, POOL_W), BF16),
                        pltpu.VMEM((D // 128, tm, 128), F32)],
        compiler_params=_params(2),
        name="out_proj",
    )(x, o_f, o_b, gg, swa, pu, pu, pu, icnt, mod, gng, ones96, poolw, pools, wout, g2)


def _ffn_kernel(h_ref, hp_ref, hn_ref, x_ref, mod_ref, wup_ref, cwb_ref, wdn_ref,
                o_ref, hx_ref, u_ref, act_ref, nat_ref, *, tm):
    i = pl.program_id(0)
    nt = pl.num_programs(0)
    H = FFN_HALO
    N = N_FF_CHUNKS
    K = TOKEN_RESIDUES
    J = tm // K
    hx_ref[0:tm, :] = h_ref[0]
    before = jnp.where(i > 0, hp_ref[0].astype(F32)[H - 1:H], 0.0)
    after = jnp.where(i < nt - 1, hn_ref[0].astype(F32)[0:1], 0.0)
    halo_row = lax.broadcasted_iota(jnp.int32, (H, D_MODEL), 0)
    hx_ref[tm:, :] = jnp.where(halo_row == 0, before, jnp.where(halo_row == 1, after, 0.0)).astype(BF16)
    row8 = lax.broadcasted_iota(jnp.int32, (8, 128), 0)

    def up(fc, slot):
        for part in range(2):
            c0 = part * D_FF + fc * FF_CHUNK
            u_ref[slot, part] = _dot(hx_ref[...], wup_ref[:, c0:c0 + FF_CHUNK])

    def conv_act(fc, slot):
        for lt in range(FF_CHUNK // 128):
            ls = slice(lt * 128, (lt + 1) * 128)
            wts = [cwb_ref[part, fc, :, ls] for part in range(2)]
            for k in range(K):
                taps = []
                for part in range(2):
                    u = u_ref.at[slot, part]
                    if k > 0:
                        prev = u[(k - 1) * J:k * J, ls]
                    else:
                        rolled = pltpu.roll(u[(K - 1) * J:K * J, ls], 1, 0)
                        head = jnp.where(row8 == 0, u[tm:tm + 1, ls], rolled[0:8])
                        prev = jnp.concatenate([head, rolled[8:]], axis=0)
                    if k < K - 1:
                        nxt = u[(k + 1) * J:(k + 2) * J, ls]
                    else:
                        rolled = pltpu.roll(u[0:J, ls], J - 1, 0)
                        tail = jnp.where(row8 == 7, u[tm + 1:tm + 2, ls], rolled[J - 8:J])
                        nxt = jnp.concatenate([rolled[:J - 8], tail], axis=0)
                    w = wts[part]
                    taps.append(prev * w[0:1] + u[k * J:(k + 1) * J, ls] * w[1:2] + nxt * w[2:3] + w[3:4])
                a, g = taps
                act_ref[fc, k * J:(k + 1) * J, ls] = ((g * _sigmoid(g)) * a).astype(BF16)

    up(0, 0)
    for s in range(1, N + 1):
        if s < N:
            up(s, s % 2)
        conv_act(s - 1, (s - 1) % 2)
    act = jnp.concatenate([act_ref[c] for c in range(N)], axis=1)
    y = mod_ref[0, 5:6, :] * _dot(act, wdn_ref[...])
    for lt in range(D_MODEL // 128):
        ls = slice(lt * 128, (lt + 1) * 128)
        for k in range(K):
            nat_ref[lt, pl.ds(k, J, stride=K), :] = y[k * J:(k + 1) * J, ls]
        o_ref[0, :, ls] = x_ref[0, :, ls] + nat_ref[lt]


def _ffn(h2, x1, mod, mod_row, wup, cwb, wdn, *, tm):
    B, T, D = x1.shape
    nt = T // tm
    H = FFN_HALO
    nh = T // H

    def tok(width):
        return pl.BlockSpec((1, tm, width), lambda i, b: (b, i, 0))

    def const(arr):
        nd = arr.ndim
        return pl.BlockSpec(arr.shape, lambda i, b: (0,) * nd)

    prev = pl.BlockSpec((1, H, D), lambda i, b: (b, jnp.maximum(i * (tm // H) - 1, 0), 0))
    nxt = pl.BlockSpec((1, H, D), lambda i, b: (b, jnp.minimum((i + 1) * (tm // H), nh - 1), 0))
    return pl.pallas_call(
        functools.partial(_ffn_kernel, tm=tm),
        grid=(nt, B),
        in_specs=[tok(D), prev, nxt, tok(D),
                  pl.BlockSpec((1, 6, D), lambda i, b: (mod_row(b), 0, 0)),
                  const(wup), const(cwb), const(wdn)],
        out_specs=tok(D),
        out_shape=jax.ShapeDtypeStruct((B, T, D), F32),
        scratch_shapes=[pltpu.VMEM((tm + H, D), BF16),
                        pltpu.VMEM((2, 2, tm + H, FF_CHUNK), F32),
                        pltpu.VMEM((N_FF_CHUNKS, tm, FF_CHUNK), BF16),
                        pltpu.VMEM((D // 128, tm, 128), F32)],
        compiler_params=_params(2),
        name="ffn",
    )(h2, h2, h2, x1, mod, wup, cwb, wdn)


def _pad_heads(a, n_heads, width, padded):
    a = a.reshape(a.shape[:-1] + (n_heads, width))
    a = jnp.pad(a, [(0, 0)] * (a.ndim - 1) + [(0, padded - width)])
    return a.reshape(a.shape[:-2] + (n_heads * padded,))


def _permute_heads(a, axis):
    shape = a.shape
    axis = axis % a.ndim
    a = a.reshape(shape[:axis] + (SWA_HEADS, HEAD_DIM) + shape[axis + 1:])
    a = jnp.take(a, jnp.array(SWA_HEAD_ORDER), axis=axis)
    return a.reshape(shape)


def _block_diag_ones(n, block):
    idx = np.arange(n) // block
    return jnp.asarray(idx[:, None] == idx[None, :], BF16)


def _pack_params(w_in, gla_w_dec, gla_b_dec, q_norm_g, k_norm_g, pool_w, w_out, conv_w, conv_b):
    depth = w_in.shape[0]
    gq, gk, gv, gg, zf, zb, aq, ak, av, pu = jnp.split(
        w_in, np.cumsum([192, 192, 384, 384, 16, 16, 384, 128, 128])[:].tolist(), axis=-1)
    z = jnp.pad(jnp.concatenate([zf, zb], axis=-1), ((0, 0), (0, 0), (0, Z_PAD - 2 * GLA_RANK)))
    w = jnp.concatenate([_pad_heads(gq, GLA_HEADS, GLA_DK, GLA_DKP),
                         _pad_heads(gk, GLA_HEADS, GLA_DK, GLA_DKP),
                         gv, gg, z, av, _permute_heads(aq, -1), ak, pu], axis=-1).astype(BF16)
    dec = _pad_heads(gla_w_dec, GLA_HEADS, GLA_DK, GLA_DKP)
    none = jnp.zeros_like(dec[:, 0])
    wdec = jnp.concatenate([jnp.concatenate([dec[:, 0], none], axis=-1),
                            jnp.concatenate([none, dec[:, 1]], axis=-1)], axis=-2)
    wdec = jnp.pad(wdec, ((0, 0), (0, Z_PAD - 2 * GLA_RANK), (0, 0))).astype(BF16)
    bdec = _pad_heads(gla_b_dec, GLA_HEADS, GLA_DK, GLA_DKP).reshape(depth, 1, 2 * GLA_QKP)
    qkg = jnp.concatenate([jnp.tile(q_norm_g, (1, SWA_HEADS)) * (HEAD_DIM ** -0.5 * LOG2E),
                           jnp.tile(k_norm_g, (1, SWA_KV_HEADS))], axis=-1)[:, None, :]
    n_groups = len(POOL_WINDOWS)
    poolw = jnp.concatenate(
        [jnp.pad(pool_w[:, g], ((0, 0), (0, 0), (g * POOL_GROUP, (n_groups - 1 - g) * POOL_GROUP)))
         for g in range(n_groups)], axis=-2).astype(BF16)
    wout = jnp.concatenate([w_out[:, :GLA_V], _permute_heads(w_out[:, GLA_V:GLA_V + SWA_Q], -2),
                            w_out[:, GLA_V + SWA_Q:]], axis=-2).astype(BF16)
    cwb = jnp.concatenate([conv_w, conv_b[:, None, :]], axis=-2).reshape(depth, 4, 2, N_FF_CHUNKS, FF_CHUNK)
    return dict(w=w, wdec=wdec, bdec=bdec, qkg=qkg, poolw=poolw, wout=wout,
                cwb=jnp.transpose(cwb, (0, 2, 3, 1, 4)))


def _rope_tables(n_tokens):
    rows_n = n_tokens // GRID_W
    rows = jnp.repeat(jnp.arange(rows_n), GRID_W).astype(F32)
    cols = jnp.tile(jnp.arange(GRID_W), rows_n).astype(F32)
    nf = HEAD_DIM // 4
    inv = ROPE_BASE ** (-jnp.arange(nf, dtype=F32) / nf)
    ar = rows[:, None] * inv
    ac = cols[:, None] * inv
    cos = jnp.concatenate([jnp.cos(ar), jnp.cos(ar), jnp.cos(ac), jnp.cos(ac)], axis=-1)
    sin = jnp.concatenate([-jnp.sin(ar), jnp.sin(ar), -jnp.sin(ac), jnp.sin(ac)], axis=-1)
    return jnp.tile(cos, (1, 128 // HEAD_DIM)), jnp.tile(sin, (1, 128 // HEAD_DIM))


def _pool_inv_counts(n_tokens):
    t = jnp.arange(n_tokens)[:, None]
    half = jnp.repeat(jnp.asarray(POOL_WINDOWS) // 2, POOL_GROUP)[None, :]
    cnt = jnp.minimum(t + half, n_tokens) - jnp.maximum(t - half, 0)
    return 1.0 / cnt.astype(F32)


def _tile_sizes(T):
    return dict(ti=min(T, 1024), tg=min(T, 1024), tm=min(T, 512))


def kernel(x, c, ctx, c_ctx, w_ada, b_ada, norm1_g, w_in, gla_w_dec, gla_b_dec, gla_norm_g, q_norm_g,
           k_norm_g, sink_logit, pool_w, pool_scale, w_out, norm2_g, w_up, conv_w, conv_b, w_down):
    B, S, D = x.shape
    L = ctx.shape[1]
    c_rows = jnp.zeros((8, D), F32).at[:B].set(c).at[B].set(c_ctx)
    mod_all = _modulation(c_rows, w_ada, b_ada).reshape(DEPTH, 8, 6, D)
    cos, sin = _rope_tables(S)
    chunk_ones = np.ones((GLA_CHUNK, GLA_CHUNK), np.float32)
    tri = jnp.asarray(np.stack([np.tril(chunk_ones), np.triu(chunk_ones)]), BF16)
    ones64 = _block_diag_ones(SWA_Q + SWA_KV, HEAD_DIM) * jnp.asarray(1.0 / HEAD_DIM, BF16)
    ones96 = _block_diag_ones(GLA_V, GLA_DV)
    icnt_lat, icnt_ctx = _pool_inv_counts(S), _pool_inv_counts(L)
    bd = jnp.asarray((np.arange(GLA_V) // GLA_DV)[:, None] == (np.arange(GLA_QKP) // GLA_DKP)[None, :], F32)
    lat, cx = _tile_sizes(S), _tile_sizes(L)
    lat_row = lambda b: b
    ctx_row = lambda b: B
    zero_state = jnp.zeros((B, 2, GLA_V, GLA_QKP), F32)

    packed = _pack_params(w_in, gla_w_dec, gla_b_dec, q_norm_g, k_norm_g, pool_w, w_out, conv_w, conv_b)
    w_up_bf, w_down_bf = w_up.astype(BF16), w_down.astype(BF16)

    for l in range(DEPTH):
        p = {name: arr[l] for name, arr in packed.items()}
        mod = mod_all[l]
        g1 = norm1_g[l][None, :]
        g2 = norm2_g[l][None, :]
        gng = jnp.tile(gla_norm_g[l], GLA_HEADS)[None, :]
        pools = pool_scale[l][None, :]
        wup = w_up_bf[l]
        wdn = w_down_bf[l]
        shared = (g1, p["w"], p["wdec"], p["bdec"], tri, ones64, p["qkg"])
        update_ctx = l < DEPTH - 1

        (cqdf, ckif, ckef, cqdb, ckib, ckeb, cdecf, cdecb, cgv, cgg, csq, cskv, cpu) = _in_proj(
            ctx, mod, ctx_row, *shared, cos[:L], sin[:L], tm=cx["ti"], rope=False)
        co_f, co_b, st = _gla(cqdf, ckif, ckef, cqdb, ckib, ckeb, cdecf, cdecb, cgv, zero_state, bd,
                              tg=cx["tg"])

        (qdf, kif, kef, qdb, kib, keb, decf, decb, gv, gg, sq, skv, pu) = _in_proj(
            x, mod, lat_row, *shared, cos, sin, tm=lat["ti"], rope=True)
        o_f, o_b, _ = _gla(qdf, kif, kef, qdb, kib, keb, decf, decb, gv, st, bd, tg=lat["tg"])
        swa = _swa(sq, skv, cskv, sink_logit[l], local=True)
        x1, h2 = _out_proj(x, o_f, o_b, gg, swa, pu, icnt_lat, mod, lat_row, gng, ones96, p["poolw"],
                           pools, p["wout"], g2, tm=lat["tm"])
        x = _ffn(h2, x1, mod, lat_row, wup, p["cwb"], wdn, tm=lat["tm"])

        if update_ctx:
            cswa = _swa(csq, None, cskv, sink_logit[l], local=False)
            c1, ch2 = _out_proj(ctx, co_f, co_b, cgg, cswa, cpu, icnt_ctx, mod, ctx_row, gng, ones96,
                                p["poolw"], pools, p["wout"], g2, tm=cx["tm"])
            ctx = _ffn(ch2, c1, mod, ctx_row, wup, p["cwb"], wdn, tm=cx["tm"])
    return x
```

```python
import functools

import jax
import jax.numpy as jnp
import numpy as np
from jax import lax
from jax.experimental import pallas as pl
from jax.experimental.pallas import tpu as pltpu

F32 = jnp.float32
BF16 = jnp.bfloat16

LANES = 128
SUBLANES = 8
N_MOD = 6
MOD_ROWS = SUBLANES

D_MODEL = 1024
DEPTH = 2
GRID_W = 64
GLA_HEADS = 4
GLA_DK = 48
GLA_DKP = 64
GLA_DV = 96
GLA_RANK = 16
GLA_TAU = 16.0
GLA_CHUNK = 64
SWA_HEADS = 6
SWA_KV_HEADS = 2
SWA_GROUP = SWA_HEADS // SWA_KV_HEADS
HEAD_DIM = 64
WINDOW = 128
Q_BLOCK = 128
SWA_SUB = 8
ROPE_BASE = 10000.0
ROPE_NF = HEAD_DIM // 4
POOL_WINDOWS = (2, 4, 8, 16)
POOL_GROUP = 64
POOL_HALF_MAX = max(POOL_WINDOWS) // 2
POOL_ROWS = 128
D_FF = 2816
EPS = 1e-6
NEG_INF = -1e30
LOG2E = 1.4426950408889634

GLA_QKP = GLA_HEADS * GLA_DKP
GLA_V = GLA_HEADS * GLA_DV
SWA_Q = SWA_HEADS * HEAD_DIM
SWA_KV = SWA_KV_HEADS * HEAD_DIM
POOL_W = len(POOL_WINDOWS) * POOL_GROUP
Z_PAD = LANES

OFF_Q = 0
OFF_K = OFF_Q + GLA_QKP
OFF_GV = OFF_K + GLA_QKP
OFF_GG = OFF_GV + GLA_V
OFF_Z = OFF_GG + GLA_V
OFF_SV = OFF_Z + Z_PAD
OFF_SQ = OFF_SV + SWA_KV
OFF_SK = OFF_SQ + SWA_Q
OFF_PU = OFF_SK + SWA_KV
IN_WP = OFF_PU + POOL_W

SWA_HEAD_ORDER = tuple(h for t in range(SWA_GROUP) for h in (t, t + SWA_GROUP))

FF_CHUNK = 256
N_FF_CHUNKS = D_FF // FF_CHUNK
FFN_HALO = 16
TOKEN_RESIDUES = SUBLANES
V7X_VMEM_LIMIT = 56 * 1024 * 1024

_NT = (((1,), (1,)), ((), ()))
_TN = (((0,), (0,)), ((), ()))


def _dot(a, b):
    return jnp.dot(a, b, preferred_element_type=F32)


def _params():
    return pltpu.CompilerParams(dimension_semantics=("arbitrary", "arbitrary"),
                                vmem_limit_bytes=V7X_VMEM_LIMIT)


def _sigmoid(x):
    return 1.0 / (1.0 + jnp.exp2(x * -LOG2E))


def _log_sigmoid(x):
    return jnp.minimum(x, 0.0) - jnp.log(1.0 + jnp.exp2(jnp.abs(x) * -LOG2E))


def _rms(x):
    return x * lax.rsqrt(jnp.mean(x * x, axis=-1, keepdims=True) + EPS)


def _mod_kernel(c_ref, w_ref, b_ref, o_ref):
    c = c_ref[...]
    act = (c * _sigmoid(c)).astype(BF16)
    o_ref[0] = _dot(act, w_ref[0].astype(BF16)) + b_ref[0]


def _modulation(c_rows, w_ada, b_ada):
    n = w_ada.shape[-1]
    tn = n // 4
    return pl.pallas_call(
        _mod_kernel,
        grid=(DEPTH, n // tn),
        in_specs=[pl.BlockSpec((MOD_ROWS, D_MODEL), lambda l, j: (0, 0)),
                  pl.BlockSpec((1, D_MODEL, tn), lambda l, j: (l, 0, j)),
                  pl.BlockSpec((1, 1, tn), lambda l, j: (l, 0, j))],
        out_specs=pl.BlockSpec((1, MOD_ROWS, tn), lambda l, j: (l, 0, j)),
        out_shape=jax.ShapeDtypeStruct((DEPTH, MOD_ROWS, n), F32),
        compiler_params=_params(),
        name="adaln_mod",
    )(c_rows, w_ada, b_ada.reshape(DEPTH, 1, n))


def _in_proj_kernel(x_ref, mod_ref, g1_ref, w_ref, wdec_ref, bdec_ref, tri_ref, ones_ref, qkg_ref,
                    cos_ref, sin_ref,
                    qdf_ref, kif_ref, kef_ref, qdb_ref, kib_ref, keb_ref, decf_ref, decb_ref,
                    gv_ref, gg_ref, sq_ref, skv_ref, pu_ref, *, tm, rope):
    x = x_ref[0]
    sh1 = mod_ref[0, 0:1, :]
    sc1 = mod_ref[0, 1:2, :]
    h = _rms(x) * (g1_ref[...] * (1.0 + sc1)) + sh1
    hb = h.astype(BF16)

    qk = _dot(hb, w_ref[:, OFF_Q:OFF_GV])
    q = qk[:, :GLA_QKP] * (GLA_DK ** -0.5)
    k = qk[:, GLA_QKP:]
    zsv = _dot(hb, w_ref[:, OFF_Z:OFF_SQ])
    z = zsv[:, :Z_PAD]
    skv_ref[0, :, SWA_KV:] = zsv[:, Z_PAD:].astype(BF16)
    la = _log_sigmoid(_dot(z.astype(BF16), wdec_ref[...]) + bdec_ref[...]) * (LOG2E / GLA_TAU)
    la_hi = la.astype(BF16)
    la_lo = (la - la_hi.astype(F32)).astype(BF16)
    lower = tri_ref[0]
    upper = tri_ref[1]
    C = GLA_CHUNK
    for c in range(tm // C):
        r = slice(c * C, (c + 1) * C)
        bcf = _dot(lower, la_hi[r, :GLA_QKP]) + _dot(lower, la_lo[r, :GLA_QKP])
        bcb = _dot(upper, la_hi[r, GLA_QKP:]) + _dot(upper, la_lo[r, GLA_QKP:])
        blf = bcf[C - 1:C, :]
        blb = bcb[0:1, :]
        qc = q[r]
        kc = k[r]
        qdf_ref[0, r, :] = (qc * jnp.exp2(bcf)).astype(BF16)
        kif_ref[0, r, :] = (kc * jnp.exp2(-bcf)).astype(BF16)
        kef_ref[0, r, :] = (kc * jnp.exp2(blf - bcf)).astype(BF16)
        qdb_ref[0, r, :] = (qc * jnp.exp2(bcb)).astype(BF16)
        kib_ref[0, r, :] = (kc * jnp.exp2(-bcb)).astype(BF16)
        keb_ref[0, r, :] = (kc * jnp.exp2(blb - bcb)).astype(BF16)
        decf_ref[0, c] = jnp.exp2(blf)
        decb_ref[0, c] = jnp.exp2(blb)

    vg = _dot(hb, w_ref[:, OFF_GV:OFF_Z])
    gv_ref[0] = vg[:, :GLA_V].astype(BF16)
    gg_ref[0] = vg[:, GLA_V:].astype(BF16)

    sqk = _dot(hb, w_ref[:, OFF_SQ:OFF_PU])
    ss = _dot((sqk * sqk).astype(BF16), ones_ref[...])
    sqk = sqk * lax.rsqrt(ss + EPS) * qkg_ref[...]
    tiles = []
    for t in range((SWA_Q + SWA_KV) // LANES):
        xt = sqk[:, t * LANES:(t + 1) * LANES]
        if rope:
            lane = lax.broadcasted_iota(jnp.int32, xt.shape, 1)
            first = (lane % (2 * ROPE_NF)) < ROPE_NF
            partner = jnp.where(first, pltpu.roll(xt, LANES - ROPE_NF, 1), pltpu.roll(xt, ROPE_NF, 1))
            xt = xt * cos_ref[...] + partner * sin_ref[...]
        tiles.append(xt.astype(BF16))
    for t in range(SWA_Q // LANES):
        sq_ref[0, :, t * LANES:(t + 1) * LANES] = tiles[t]
    skv_ref[0, :, :SWA_KV] = tiles[-1]

    pu_ref[0] = _dot(hb, w_ref[:, OFF_PU:IN_WP])


def _in_proj(x, mod, mod_row, g1, w, wdec, bdec, tri, ones64, qkg, cos, sin, *, tm, rope):
    B, T, D = x.shape
    nt = T // tm
    nc = T // GLA_CHUNK

    def tok(width, dtype):
        return (jax.ShapeDtypeStruct((B, T, width), dtype),
                pl.BlockSpec((1, tm, width), lambda i, b: (b, i, 0)))

    def const(arr):
        nd = arr.ndim
        return pl.BlockSpec(arr.shape, lambda i, b: (0,) * nd)

    dec = (jax.ShapeDtypeStruct((B, nc, 1, GLA_QKP), F32),
           pl.BlockSpec((1, tm // GLA_CHUNK, 1, GLA_QKP), lambda i, b: (b, i, 0, 0)))
    outs = [tok(GLA_QKP, BF16)] * 6 + [dec, dec] + [tok(GLA_V, BF16), tok(GLA_V, BF16),
                                                   tok(SWA_Q, BF16), tok(2 * SWA_KV, BF16),
                                                   tok(POOL_W, F32)]
    return pl.pallas_call(
        functools.partial(_in_proj_kernel, tm=tm, rope=rope),
        grid=(nt, B),
        in_specs=[pl.BlockSpec((1, tm, D), lambda i, b: (b, i, 0)),
                  pl.BlockSpec((1, N_MOD, D), lambda i, b: (mod_row(b), 0, 0)),
                  const(g1), const(w), const(wdec), const(bdec), const(tri), const(ones64), const(qkg),
                  pl.BlockSpec((tm, LANES), lambda i, b: (i, 0)),
                  pl.BlockSpec((tm, LANES), lambda i, b: (i, 0))],
        out_specs=[o[1] for o in outs],
        out_shape=[o[0] for o in outs],
        compiler_params=_params(),
        name="in_proj",
    )(x, mod, g1, w, wdec, bdec, tri, ones64, qkg, cos, sin)


def _gla_kernel(qdf_ref, kif_ref, kef_ref, vf_ref, decf_ref, qdb_ref, kib_ref, keb_ref, vb_ref, decb_ref,
                s0_ref, bd_ref, of_ref, ob_ref, sfin_ref, st_ref, *, tg):
    i = pl.program_id(1)
    nb = pl.num_programs(1)
    C = GLA_CHUNK
    ncb = tg // C

    @pl.when(i == 0)
    def _():
        st_ref[...] = s0_ref[0]

    HC = GLA_HEADS * C
    row_head = lax.broadcasted_iota(jnp.int32, (HC, GLA_QKP), 0) // C
    k_mask = row_head == lax.broadcasted_iota(jnp.int32, (HC, GLA_QKP), 1) // GLA_DKP
    vlane = lax.broadcasted_iota(jnp.int32, (HC, GLA_V), 1)
    vhead = sum((vlane >= h * GLA_DV).astype(jnp.int32) for h in range(1, GLA_HEADS))
    v_mask = lax.broadcasted_iota(jnp.int32, (HC, GLA_V), 0) // C == vhead
    row_i = lax.broadcasted_iota(jnp.int32, (C, HC), 0)
    col_j = lax.broadcasted_iota(jnp.int32, (C, HC), 1) % C
    bd = bd_ref[...]

    def chunk(d, qd_ref, ki_ref, ke_ref, v_ref, dec_ref, o_ref, c):
        r = slice(c * C, (c + 1) * C)
        qd = qd_ref[0, r, :]
        ki = ki_ref[0, r, :]
        ke = ke_ref[0, r, :]
        vv = v_ref[0, r, :]
        k_bd = jnp.where(k_mask, jnp.concatenate([ki] * GLA_HEADS, axis=0), jnp.zeros((), BF16))
        v_bd = jnp.where(v_mask, jnp.concatenate([vv] * GLA_HEADS, axis=0), jnp.zeros((), BF16))
        st = st_ref[d]
        rhs = jnp.concatenate([k_bd, st.astype(BF16)], axis=0)
        res = lax.dot_general(qd, rhs, _NT, preferred_element_type=F32)
        keep = (col_j <= row_i) if d == 0 else (col_j >= row_i)
        p = jnp.where(keep, res[:, :HC], 0.0).astype(BF16)
        o = res[:, HC:] + _dot(p, v_bd)
        o_ref[0, r, :] = o.astype(BF16)
        upd = lax.dot_general(vv, ke, _TN, preferred_element_type=F32)
        st_ref[d] = st * dec_ref[0, c] + upd * bd

    for c in range(ncb):
        chunk(0, qdf_ref, kif_ref, kef_ref, vf_ref, decf_ref, of_ref, c)
        chunk(1, qdb_ref, kib_ref, keb_ref, vb_ref, decb_ref, ob_ref, ncb - 1 - c)

    @pl.when(i == nb - 1)
    def _():
        sfin_ref[0] = st_ref[...]


def _gla(qdf, kif, kef, qdb, kib, keb, decf, decb, v, s0, bd, *, tg):
    B, T, _ = v.shape
    nb = T // tg
    ncb = tg // GLA_CHUNK

    def fwd(width):
        return pl.BlockSpec((1, tg, width), lambda b, i: (b, i, 0))

    def bwd(width):
        return pl.BlockSpec((1, tg, width), lambda b, i: (b, nb - 1 - i, 0))

    dec_f = pl.BlockSpec((1, ncb, 1, GLA_QKP), lambda b, i: (b, i, 0, 0))
    dec_b = pl.BlockSpec((1, ncb, 1, GLA_QKP), lambda b, i: (b, nb - 1 - i, 0, 0))
    st_spec = pl.BlockSpec((1, 2, GLA_V, GLA_QKP), lambda b, i: (b, 0, 0, 0))
    return pl.pallas_call(
        functools.partial(_gla_kernel, tg=tg),
        grid=(B, nb),
        in_specs=[fwd(GLA_QKP), fwd(GLA_QKP), fwd(GLA_QKP), fwd(GLA_V), dec_f,
                  bwd(GLA_QKP), bwd(GLA_QKP), bwd(GLA_QKP), bwd(GLA_V), dec_b,
                  st_spec, pl.BlockSpec(bd.shape, lambda b, i: (0, 0))],
        out_specs=[fwd(GLA_V), bwd(GLA_V), st_spec],
        out_shape=[jax.ShapeDtypeStruct((B, T, GLA_V), BF16),
                   jax.ShapeDtypeStruct((B, T, GLA_V), BF16),
                   jax.ShapeDtypeStruct((B, 2, GLA_V, GLA_QKP), F32)],
        scratch_shapes=[pltpu.VMEM((2, GLA_V, GLA_QKP), F32)],
        compiler_params=_params(),
        name="gla",
    )(qdf, kif, kef, v, decf, qdb, kib, keb, v, decb, s0, bd)


def _swa_kernel(*refs, n_local, n_sub, seq_len):
    if n_local:
        q_ref, *loc_refs, ckv_ref, sink_ref, o_ref = refs
    else:
        q_ref, ckv_ref, sink_ref, o_ref = refs
        loc_refs = []
    QB = Q_BLOCK
    lane = lax.broadcasted_iota(jnp.int32, (QB, LANES), 1)
    low = lane < HEAD_DIM
    n_loc = n_local * QB
    for sub in range(n_sub):
        _swa_block(pl.program_id(1) * n_sub + sub, sub, q_ref, loc_refs[sub:sub + n_local] + [ckv_ref],
                   sink_ref, o_ref, low, n_local=n_local, n_loc=n_loc, seq_len=seq_len)


def _swa_block(n, sub, q_ref, kv_parts, sink_ref, o_ref, low, *, n_local, n_loc, seq_len):
    QB = Q_BLOCK
    rows = slice(sub * QB, (sub + 1) * QB)
    kcat = jnp.concatenate([r[0, :, :SWA_KV] for r in kv_parts], axis=0)
    vcat = jnp.concatenate([r[0, :, SWA_KV:] for r in kv_parts], axis=0)
    if n_local:
        qi = lax.broadcasted_iota(jnp.int32, (QB, n_loc), 0)
        kj = lax.broadcasted_iota(jnp.int32, (QB, n_loc), 1)
        k_pos = (n - 1) * QB + kj
        dist = kj - QB - qi
        valid = (dist >= -WINDOW) & (dist <= WINDOW) & (k_pos >= 0) & (k_pos < seq_len)

    def scores(g):
        slabs = []
        for t in range(SWA_GROUP):
            qt = q_ref[0, rows, t * LANES:(t + 1) * LANES]
            slabs.append(jnp.where(low if g == 0 else ~low, qt, jnp.zeros_like(qt)))
        return lax.dot_general(jnp.concatenate(slabs, axis=0), kcat, _NT,
                               preferred_element_type=F32)

    def attend(g, s_grp):
        ps, dens = [], []
        for t in range(SWA_GROUP):
            sink = sink_ref[g * SWA_GROUP + t] * LOG2E
            sh = s_grp[t * QB:(t + 1) * QB]
            s_ctx = sh[:, n_loc:]
            m = jnp.maximum(jnp.max(s_ctx, axis=-1, keepdims=True), sink)
            if n_local:
                s_loc = jnp.where(valid, sh[:, :n_loc], NEG_INF)
                m = jnp.maximum(m, jnp.max(s_loc, axis=-1, keepdims=True))
                p_loc = jnp.exp2(s_loc - m)
            p_ctx = jnp.exp2(s_ctx - m)
            den = jnp.sum(p_ctx, axis=-1, keepdims=True) + jnp.exp2(sink - m)
            if n_local:
                den = den + jnp.sum(p_loc, axis=-1, keepdims=True)
                ps.append(jnp.concatenate([p_loc, p_ctx], axis=1).astype(BF16))
            else:
                ps.append(p_ctx.astype(BF16))
            dens.append(den)
        o_grp = _dot(jnp.concatenate(ps, axis=0), vcat)
        return [o_grp[t * QB:(t + 1) * QB] / dens[t] for t in range(SWA_GROUP)]

    s_groups = [scores(g) for g in range(SWA_KV_HEADS)]
    o_a = attend(0, s_groups[0])
    o_b = attend(1, s_groups[1])
    for t in range(SWA_GROUP):
        o_ref[0, rows, t * LANES:(t + 1) * LANES] = jnp.where(low, o_a[t], o_b[t]).astype(BF16)


def _swa(q, kv, ckv, sink, *, local):
    B, S, _ = q.shape
    L = ckv.shape[1]
    nq = S // Q_BLOCK
    n_sub = min(SWA_SUB, nq)
    qspec = pl.BlockSpec((1, n_sub * Q_BLOCK, SWA_Q), lambda b, n: (b, n, 0))
    cspec = pl.BlockSpec((1, L, 2 * SWA_KV), lambda b, n: (b, 0, 0))
    sspec = pl.BlockSpec(memory_space=pltpu.SMEM)
    if local:
        def blk(off):
            return pl.BlockSpec((1, Q_BLOCK, 2 * SWA_KV),
                                lambda b, n: (b, jnp.clip(n * n_sub + off, 0, nq - 1), 0))
        offs = range(-1, n_sub + 1)
        in_specs = [qspec] + [blk(off) for off in offs] + [cspec, sspec]
        args = (q,) + (kv,) * len(offs) + (ckv, sink)
    else:
        in_specs = [qspec, cspec, sspec]
        args = (q, ckv, sink)
    return pl.pallas_call(
        functools.partial(_swa_kernel, n_local=3 if local else 0, n_sub=n_sub, seq_len=S),
        grid=(B, nq // n_sub),
        in_specs=in_specs,
        out_specs=qspec,
        out_shape=jax.ShapeDtypeStruct((B, S, SWA_Q), BF16),
        compiler_params=_params(),
        name="swa",
    )(*args)


def _out_kernel(x_ref, of_ref, ob_ref, gg_ref, swa_ref, pu_ref, pup_ref, pun_ref, icnt_ref, mod_ref,
                gng_ref, ones_ref, poolw_ref, pools_ref, wout_ref, g2_ref,
                x1_ref, h2_ref, ext_ref, cen_ref, hs_ref, *, tm):
    i = pl.program_id(0)
    nt = pl.num_programs(0)
    HP = POOL_HALF_MAX

    o = of_ref[0].astype(F32) + ob_ref[0].astype(F32)
    ss = _dot((o * o).astype(BF16), ones_ref[...])
    gg = gg_ref[0].astype(F32)
    gla = o * lax.rsqrt(ss * (1.0 / GLA_DV) + EPS) * gng_ref[...] * (gg * _sigmoid(gg))

    ext_ref[0:HP, :] = jnp.where(i > 0, pup_ref[0], 0.0)
    ext_ref[HP:HP + tm, :] = pu_ref[0]
    ext_ref[HP + tm:, :] = jnp.where(i < nt - 1, pun_ref[0], 0.0)
    upper_group = lax.broadcasted_iota(jnp.int32, (POOL_ROWS, LANES), 1) >= POOL_GROUP
    for rb in range(tm // POOL_ROWS):
        r0 = HP + rb * POOL_ROWS
        rows = slice(rb * POOL_ROWS, (rb + 1) * POOL_ROWS)
        for tile in range(POOL_W // LANES):
            ls = slice(tile * LANES, (tile + 1) * LANES)
            n_rows = POOL_ROWS + 2 * HP
            run = ext_ref[r0 - HP:r0 + POOL_ROWS + HP, ls]
            wins = []
            span = 1
            for w in POOL_WINDOWS:
                while span < w:
                    run = run + pltpu.roll(run, span, 0)
                    span *= 2
                if w in POOL_WINDOWS[2 * tile:2 * tile + 2]:
                    lead = w // 2 - 1
                    aligned = pltpu.roll(run, n_rows - lead, 0) if lead else run
                    wins.append(aligned[HP:HP + POOL_ROWS])
                if len(wins) == 2:
                    break
            win = jnp.where(upper_group, wins[1], wins[0])
            centred = win * icnt_ref[rows, ls] - ext_ref[r0:r0 + POOL_ROWS, ls]
            cen_ref[rows, ls] = centred.astype(BF16)
    pool = _dot(cen_ref[...], poolw_ref[...]) * pools_ref[...]

    cat = jnp.concatenate([gla.astype(BF16), swa_ref[0], pool.astype(BF16)], axis=1)
    y = _dot(cat, wout_ref[...])
    g1 = mod_ref[0, 2:3, :]
    sh2 = mod_ref[0, 3:4, :]
    sc2 = mod_ref[0, 4:5, :]
    x1 = x_ref[0] + g1 * y
    x1_ref[0] = x1
    h2 = _rms(x1) * (g2_ref[...] * (1.0 + sc2)) + sh2
    J = tm // TOKEN_RESIDUES
    for lt in range(D_MODEL // LANES):
        ls = slice(lt * LANES, (lt + 1) * LANES)
        hs_ref[lt] = h2[:, ls]
        for k in range(TOKEN_RESIDUES):
            h2_ref[0, k * J:(k + 1) * J, ls] = hs_ref[lt, pl.ds(k, J, stride=TOKEN_RESIDUES), :].astype(BF16)


def _out_proj(x, o_f, o_b, gg, swa, pu, icnt, mod, mod_row, gng, ones96, poolw, pools, wout, g2, *, tm):
    B, T, D = x.shape
    nt = T // tm
    HP = POOL_HALF_MAX
    nh = T // HP

    def tok(width):
        return pl.BlockSpec((1, tm, width), lambda i, b: (b, i, 0))

    def const(arr):
        nd = arr.ndim
        return pl.BlockSpec(arr.shape, lambda i, b: (0,) * nd)

    prev = pl.BlockSpec((1, HP, POOL_W), lambda i, b: (b, jnp.maximum(i * (tm // HP) - 1, 0), 0))
    nxt = pl.BlockSpec((1, HP, POOL_W), lambda i, b: (b, jnp.minimum((i + 1) * (tm // HP), nh - 1), 0))
    return pl.pallas_call(
        functools.partial(_out_kernel, tm=tm),
        grid=(nt, B),
        in_specs=[tok(D), tok(GLA_V), tok(GLA_V), tok(GLA_V), tok(SWA_Q), tok(POOL_W), prev, nxt,
                  pl.BlockSpec((tm, POOL_W), lambda i, b: (i, 0)),
                  pl.BlockSpec((1, N_MOD, D), lambda i, b: (mod_row(b), 0, 0)),
                  const(gng), const(ones96), const(poolw), const(pools), const(wout), const(g2)],
        out_specs=[tok(D), tok(D)],
        out_shape=[jax.ShapeDtypeStruct((B, T, D), F32), jax.ShapeDtypeStruct((B, T, D), BF16)],
        scratch_shapes=[pltpu.VMEM((tm + 2 * HP, POOL_W), F32), pltpu.VMEM((tm, POOL_W), BF16),
                        pltpu.VMEM((D // LANES, tm, LANES), F32)],
        compiler_params=_params(),
        name="out_proj",
    )(x, o_f, o_b, gg, swa, pu, pu, pu, icnt, mod, gng, ones96, poolw, pools, wout, g2)


def _ffn_kernel(h_ref, hp_ref, hn_ref, x_ref, mod_ref, wup_ref, cwb_ref, wdn_ref,
                o_ref, hx_ref, u_ref, act_ref, nat_ref, *, tm):
    i = pl.program_id(0)
    nt = pl.num_programs(0)
    H = FFN_HALO
    N = N_FF_CHUNKS
    K = TOKEN_RESIDUES
    J = tm // K
    hx_ref[0:tm, :] = h_ref[0]
    before = jnp.where(i > 0, hp_ref[0].astype(F32)[H - 1:H], 0.0)
    after = jnp.where(i < nt - 1, hn_ref[0].astype(F32)[0:1], 0.0)
    halo_row = lax.broadcasted_iota(jnp.int32, (H, D_MODEL), 0)
    hx_ref[tm:, :] = jnp.where(halo_row == 0, before, jnp.where(halo_row == 1, after, 0.0)).astype(BF16)
    row8 = lax.broadcasted_iota(jnp.int32, (SUBLANES, LANES), 0)

    def up(fc, slot):
        for part in range(2):
            c0 = part * D_FF + fc * FF_CHUNK
            u_ref[slot, part] = _dot(hx_ref[...], wup_ref[:, c0:c0 + FF_CHUNK])

    def conv_act(fc, slot):
        for lt in range(FF_CHUNK // LANES):
            ls = slice(lt * LANES, (lt + 1) * LANES)
            wts = [cwb_ref[part, fc, :, ls] for part in range(2)]
            for k in range(K):
                taps = []
                for part in range(2):
                    u = u_ref.at[slot, part]
                    if k > 0:
                        prev = u[(k - 1) * J:k * J, ls]
                    else:
                        rolled = pltpu.roll(u[(K - 1) * J:K * J, ls], 1, 0)
                        head = jnp.where(row8 == 0, u[tm:tm + 1, ls], rolled[:SUBLANES])
                        prev = jnp.concatenate([head, rolled[SUBLANES:]], axis=0)
                    if k < K - 1:
                        nxt = u[(k + 1) * J:(k + 2) * J, ls]
                    else:
                        rolled = pltpu.roll(u[0:J, ls], J - 1, 0)
                        tail = jnp.where(row8 == SUBLANES - 1, u[tm + 1:tm + 2, ls], rolled[J - SUBLANES:])
                        nxt = jnp.concatenate([rolled[:J - SUBLANES], tail], axis=0)
                    w = wts[part]
                    taps.append(prev * w[0:1] + u[k * J:(k + 1) * J, ls] * w[1:2] + nxt * w[2:3] + w[3:4])
                a, g = taps
                act_ref[fc, k * J:(k + 1) * J, ls] = ((g * _sigmoid(g)) * a).astype(BF16)

    up(0, 0)
    for s in range(1, N + 1):
        if s < N:
            up(s, s % 2)
        conv_act(s - 1, (s - 1) % 2)
    act = jnp.concatenate([act_ref[c] for c in range(N)], axis=1)
    y = mod_ref[0, 5:6, :] * _dot(act, wdn_ref[...])
    for lt in range(D_MODEL // LANES):
        ls = slice(lt * LANES, (lt + 1) * LANES)
        for k in range(K):
            nat_ref[lt, pl.ds(k, J, stride=K), :] = y[k * J:(k + 1) * J, ls]
        o_ref[0, :, ls] = x_ref[0, :, ls] + nat_ref[lt]


def _ffn(h2, x1, mod, mod_row, wup, cwb, wdn, *, tm):
    B, T, D = x1.shape
    nt = T // tm
    H = FFN_HALO
    nh = T // H

    def tok(width):
        return pl.BlockSpec((1, tm, width), lambda i, b: (b, i, 0))

    def const(arr):
        nd = arr.ndim
        return pl.BlockSpec(arr.shape, lambda i, b: (0,) * nd)

    prev = pl.BlockSpec((1, H, D), lambda i, b: (b, jnp.maximum(i * (tm // H) - 1, 0), 0))
    nxt = pl.BlockSpec((1, H, D), lambda i, b: (b, jnp.minimum((i + 1) * (tm // H), nh - 1), 0))
    return pl.pallas_call(
        functools.partial(_ffn_kernel, tm=tm),
        grid=(nt, B),
        in_specs=[tok(D), prev, nxt, tok(D),
                  pl.BlockSpec((1, N_MOD, D), lambda i, b: (mod_row(b), 0, 0)),
                  const(wup), const(cwb), const(wdn)],
        out_specs=tok(D),
        out_shape=jax.ShapeDtypeStruct((B, T, D), F32),
        scratch_shapes=[pltpu.VMEM((tm + H, D), BF16),
                        pltpu.VMEM((2, 2, tm + H, FF_CHUNK), F32),
                        pltpu.VMEM((N_FF_CHUNKS, tm, FF_CHUNK), BF16),
                        pltpu.VMEM((D // LANES, tm, LANES), F32)],
        compiler_params=_params(),
        name="ffn",
    )(h2, h2, h2, x1, mod, wup, cwb, wdn)


def _pad_heads(a, n_heads, width, padded):
    a = a.reshape(a.shape[:-1] + (n_heads, width))
    a = jnp.pad(a, [(0, 0)] * (a.ndim - 1) + [(0, padded - width)])
    return a.reshape(a.shape[:-2] + (n_heads * padded,))


def _permute_heads(a, axis):
    shape = a.shape
    axis = axis % a.ndim
    a = a.reshape(shape[:axis] + (SWA_HEADS, HEAD_DIM) + shape[axis + 1:])
    a = jnp.take(a, jnp.array(SWA_HEAD_ORDER), axis=axis)
    return a.reshape(shape)


def _block_diag_ones(n, block):
    idx = np.arange(n) // block
    return jnp.asarray(idx[:, None] == idx[None, :], BF16)


def _pack_params(w_in, gla_w_dec, gla_b_dec, q_norm_g, k_norm_g, pool_w, w_out, conv_w, conv_b):
    depth = w_in.shape[0]
    gq, gk, gv, gg, zf, zb, aq, ak, av, pu = jnp.split(
        w_in, np.cumsum([GLA_HEADS * GLA_DK] * 2 + [GLA_V] * 2 + [GLA_RANK] * 2
                        + [SWA_Q, SWA_KV, SWA_KV]).tolist(), axis=-1)
    z = jnp.pad(jnp.concatenate([zf, zb], axis=-1), ((0, 0), (0, 0), (0, Z_PAD - 2 * GLA_RANK)))
    w = jnp.concatenate([_pad_heads(gq, GLA_HEADS, GLA_DK, GLA_DKP),
                         _pad_heads(gk, GLA_HEADS, GLA_DK, GLA_DKP),
                         gv, gg, z, av, _permute_heads(aq, -1), ak, pu], axis=-1).astype(BF16)
    dec = _pad_heads(gla_w_dec, GLA_HEADS, GLA_DK, GLA_DKP)
    none = jnp.zeros_like(dec[:, 0])
    wdec = jnp.concatenate([jnp.concatenate([dec[:, 0], none], axis=-1),
                            jnp.concatenate([none, dec[:, 1]], axis=-1)], axis=-2)
    wdec = jnp.pad(wdec, ((0, 0), (0, Z_PAD - 2 * GLA_RANK), (0, 0))).astype(BF16)
    bdec = _pad_heads(gla_b_dec, GLA_HEADS, GLA_DK, GLA_DKP).reshape(depth, 1, 2 * GLA_QKP)
    qkg = jnp.concatenate([jnp.tile(q_norm_g, (1, SWA_HEADS)) * (HEAD_DIM ** -0.5 * LOG2E),
                           jnp.tile(k_norm_g, (1, SWA_KV_HEADS))], axis=-1)[:, None, :]
    n_groups = len(POOL_WINDOWS)
    poolw = jnp.concatenate(
        [jnp.pad(pool_w[:, g], ((0, 0), (0, 0), (g * POOL_GROUP, (n_groups - 1 - g) * POOL_GROUP)))
         for g in range(n_groups)], axis=-2).astype(BF16)
    wout = jnp.concatenate([w_out[:, :GLA_V], _permute_heads(w_out[:, GLA_V:GLA_V + SWA_Q], -2),
                            w_out[:, GLA_V + SWA_Q:]], axis=-2).astype(BF16)
    cwb = jnp.concatenate([conv_w, conv_b[:, None, :]], axis=-2).reshape(depth, 4, 2, N_FF_CHUNKS, FF_CHUNK)
    return dict(w=w, wdec=wdec, bdec=bdec, qkg=qkg, poolw=poolw, wout=wout,
                cwb=jnp.transpose(cwb, (0, 2, 3, 1, 4)))


def _rope_tables(n_tokens):
    rows_n = n_tokens // GRID_W
    rows = jnp.repeat(jnp.arange(rows_n), GRID_W).astype(F32)
    cols = jnp.tile(jnp.arange(GRID_W), rows_n).astype(F32)
    nf = ROPE_NF
    inv = ROPE_BASE ** (-jnp.arange(nf, dtype=F32) / nf)
    ar = rows[:, None] * inv
    ac = cols[:, None] * inv
    cos = jnp.concatenate([jnp.cos(ar), jnp.cos(ar), jnp.cos(ac), jnp.cos(ac)], axis=-1)
    sin = jnp.concatenate([-jnp.sin(ar), jnp.sin(ar), -jnp.sin(ac), jnp.sin(ac)], axis=-1)
    return jnp.tile(cos, (1, LANES // HEAD_DIM)), jnp.tile(sin, (1, LANES // HEAD_DIM))


def _pool_inv_counts(n_tokens):
    t = jnp.arange(n_tokens)[:, None]
    half = jnp.repeat(jnp.asarray(POOL_WINDOWS) // 2, POOL_GROUP)[None, :]
    cnt = jnp.minimum(t + half, n_tokens) - jnp.maximum(t - half, 0)
    return 1.0 / cnt.astype(F32)


def _tile_sizes(T):
    return dict(ti=min(T, 1024), tg=min(T, 1024), tm=min(T, 512))


def kernel(x, c, ctx, c_ctx, w_ada, b_ada, norm1_g, w_in, gla_w_dec, gla_b_dec, gla_norm_g, q_norm_g,
           k_norm_g, sink_logit, pool_w, pool_scale, w_out, norm2_g, w_up, conv_w, conv_b, w_down):
    B, S, D = x.shape
    L = ctx.shape[1]
    assert B + 1 <= MOD_ROWS and D == D_MODEL
    c_rows = jnp.zeros((MOD_ROWS, D), F32).at[:B].set(c).at[B].set(c_ctx)
    mod_all = _modulation(c_rows, w_ada, b_ada).reshape(DEPTH, MOD_ROWS, N_MOD, D)
    cos, sin = _rope_tables(S)
    chunk_ones = np.ones((GLA_CHUNK, GLA_CHUNK), np.float32)
    tri = jnp.asarray(np.stack([np.tril(chunk_ones), np.triu(chunk_ones)]), BF16)
    ones64 = _block_diag_ones(SWA_Q + SWA_KV, HEAD_DIM) * jnp.asarray(1.0 / HEAD_DIM, BF16)
    ones96 = _block_diag_ones(GLA_V, GLA_DV)
    icnt_lat, icnt_ctx = _pool_inv_counts(S), _pool_inv_counts(L)
    bd = jnp.asarray((np.arange(GLA_V) // GLA_DV)[:, None] == (np.arange(GLA_QKP) // GLA_DKP)[None, :], F32)
    lat, cx = _tile_sizes(S), _tile_sizes(L)
    lat_row = lambda b: b
    ctx_row = lambda b: B
    zero_state = jnp.zeros((B, 2, GLA_V, GLA_QKP), F32)

    packed = _pack_params(w_in, gla_w_dec, gla_b_dec, q_norm_g, k_norm_g, pool_w, w_out, conv_w, conv_b)
    w_up_bf, w_down_bf = w_up.astype(BF16), w_down.astype(BF16)

    for l in range(DEPTH):
        p = {name: arr[l] for name, arr in packed.items()}
        mod = mod_all[l]
        g1 = norm1_g[l][None, :]
        g2 = norm2_g[l][None, :]
        gng = jnp.tile(gla_norm_g[l], GLA_HEADS)[None, :]
        pools = pool_scale[l][None, :]
        wup = w_up_bf[l]
        wdn = w_down_bf[l]
        shared = (g1, p["w"], p["wdec"], p["bdec"], tri, ones64, p["qkg"])
        update_ctx = l < DEPTH - 1

        (cqdf, ckif, ckef, cqdb, ckib, ckeb, cdecf, cdecb, cgv, cgg, csq, cskv, cpu) = _in_proj(
            ctx, mod, ctx_row, *shared, cos[:L], sin[:L], tm=cx["ti"], rope=False)
        co_f, co_b, st = _gla(cqdf, ckif, ckef, cqdb, ckib, ckeb, cdecf, cdecb, cgv, zero_state, bd,
                              tg=cx["tg"])

        (qdf, kif, kef, qdb, kib, keb, decf, decb, gv, gg, sq, skv, pu) = _in_proj(
            x, mod, lat_row, *shared, cos, sin, tm=lat["ti"], rope=True)
        o_f, o_b, _ = _gla(qdf, kif, kef, qdb, kib, keb, decf, decb, gv, st, bd, tg=lat["tg"])
        swa = _swa(sq, skv, cskv, sink_logit[l], local=True)
        x1, h2 = _out_proj(x, o_f, o_b, gg, swa, pu, icnt_lat, mod, lat_row, gng, ones96, p["poolw"],
                           pools, p["wout"], g2, tm=lat["tm"])
        x = _ffn(h2, x1, mod, lat_row, wup, p["cwb"], wdn, tm=lat["tm"])

        if update_ctx:
            cswa = _swa(csq, None, cskv, sink_logit[l], local=False)
            c1, ch2 = _out_proj(ctx, co_f, co_b, cgg, cswa, cpu, icnt_ctx, mod, ctx_row, gng, ones96,
                                p["poolw"], pools, p["wout"], g2, tm=cx["tm"])
            ctx = _ffn(ch2, c1, mod, ctx_row, wup, p["cwb"], wdn, tm=cx["tm"])
    return x
```

```python
import functools

import jax
import jax.numpy as jnp
import numpy as np
from jax import lax
from jax.experimental import pallas as pl
from jax.experimental.pallas import tpu as pltpu

F32 = jnp.float32
BF16 = jnp.bfloat16

LANES = 128
SUBLANES = 8
N_MOD = 6
MOD_ROWS = SUBLANES

D_MODEL = 1024
DEPTH = 2
GRID_W = 64
GLA_HEADS = 4
GLA_DK = 48
GLA_DKP = 64
GLA_DV = 96
GLA_RANK = 16
GLA_TAU = 16.0
GLA_CHUNK = 64
SWA_HEADS = 6
SWA_KV_HEADS = 2
SWA_GROUP = SWA_HEADS // SWA_KV_HEADS
HEAD_DIM = 64
WINDOW = 128
Q_BLOCK = 128
SWA_SUB = 8
ROPE_BASE = 10000.0
ROPE_NF = HEAD_DIM // 4
POOL_WINDOWS = (2, 4, 8, 16)
POOL_GROUP = 64
POOL_HALF_MAX = max(POOL_WINDOWS) // 2
POOL_ROWS = 128
D_FF = 2816
EPS = 1e-6
NEG_INF = -1e30
LOG2E = 1.4426950408889634

GLA_QKP = GLA_HEADS * GLA_DKP
GLA_V = GLA_HEADS * GLA_DV
SWA_Q = SWA_HEADS * HEAD_DIM
SWA_KV = SWA_KV_HEADS * HEAD_DIM
POOL_W = len(POOL_WINDOWS) * POOL_GROUP
Z_PAD = LANES

OFF_Q = 0
OFF_K = OFF_Q + GLA_QKP
OFF_GV = OFF_K + GLA_QKP
OFF_GG = OFF_GV + GLA_V
OFF_Z = OFF_GG + GLA_V
OFF_SV = OFF_Z + Z_PAD
OFF_SQ = OFF_SV + SWA_KV
OFF_SK = OFF_SQ + SWA_Q
OFF_PU = OFF_SK + SWA_KV
IN_WP = OFF_PU + POOL_W

SWA_HEAD_ORDER = tuple(h for t in range(SWA_GROUP) for h in (t, t + SWA_GROUP))

FF_CHUNK = 256
N_FF_CHUNKS = D_FF // FF_CHUNK
FFN_HALO = 16
TOKEN_RESIDUES = SUBLANES
V7X_VMEM_LIMIT = 56 * 1024 * 1024

_NT = (((1,), (1,)), ((), ()))
_TN = (((0,), (0,)), ((), ()))


def _dot(a, b):
    return jnp.dot(a, b, preferred_element_type=F32)


def _params():
    return pltpu.CompilerParams(dimension_semantics=("arbitrary", "arbitrary"),
                                vmem_limit_bytes=V7X_VMEM_LIMIT)


def _sigmoid(x):
    return 1.0 / (1.0 + jnp.exp2(x * -LOG2E))


def _log_sigmoid(x):
    return jnp.minimum(x, 0.0) - jnp.log(1.0 + jnp.exp2(jnp.abs(x) * -LOG2E))


def _rms(x):
    return x * lax.rsqrt(jnp.mean(x * x, axis=-1, keepdims=True) + EPS)


def _mod_kernel(c_ref, w_ref, b_ref, o_ref):
    c = c_ref[...]
    act = (c * _sigmoid(c)).astype(BF16)
    o_ref[0] = _dot(act, w_ref[0].astype(BF16)) + b_ref[0]


def _modulation(c_rows, w_ada, b_ada):
    n = w_ada.shape[-1]
    tn = n // 4
    return pl.pallas_call(
        _mod_kernel,
        grid=(DEPTH, n // tn),
        in_specs=[pl.BlockSpec((MOD_ROWS, D_MODEL), lambda l, j: (0, 0)),
                  pl.BlockSpec((1, D_MODEL, tn), lambda l, j: (l, 0, j)),
                  pl.BlockSpec((1, 1, tn), lambda l, j: (l, 0, j))],
        out_specs=pl.BlockSpec((1, MOD_ROWS, tn), lambda l, j: (l, 0, j)),
        out_shape=jax.ShapeDtypeStruct((DEPTH, MOD_ROWS, n), F32),
        compiler_params=_params(),
        name="adaln_mod",
    )(c_rows, w_ada, b_ada.reshape(DEPTH, 1, n))


def _in_proj_kernel(x_ref, mod_ref, g1_ref, w_ref, wdec_ref, bdec_ref, tri_ref, ones_ref, qkg_ref,
                    cos_ref, sin_ref,
                    qdf_ref, kif_ref, kef_ref, qdb_ref, kib_ref, keb_ref, decf_ref, decb_ref,
                    gv_ref, gg_ref, sq_ref, skv_ref, pu_ref, *, tm, rope):
    x = x_ref[0]
    sh1 = mod_ref[0, 0:1, :]
    sc1 = mod_ref[0, 1:2, :]
    h = _rms(x) * (g1_ref[...] * (1.0 + sc1)) + sh1
    hb = h.astype(BF16)

    qk = _dot(hb, w_ref[:, OFF_Q:OFF_GV])
    q = qk[:, :GLA_QKP] * (GLA_DK ** -0.5)
    k = qk[:, GLA_QKP:]
    zsv = _dot(hb, w_ref[:, OFF_Z:OFF_SQ])
    z = zsv[:, :Z_PAD]
    skv_ref[0, :, SWA_KV:] = zsv[:, Z_PAD:].astype(BF16)
    la = _log_sigmoid(_dot(z.astype(BF16), wdec_ref[...]) + bdec_ref[...]) * (LOG2E / GLA_TAU)
    la_hi = la.astype(BF16)
    la_lo = (la - la_hi.astype(F32)).astype(BF16)
    lower = tri_ref[0]
    upper = tri_ref[1]
    C = GLA_CHUNK
    for c in range(tm // C):
        r = slice(c * C, (c + 1) * C)
        bcf = _dot(lower, la_hi[r, :GLA_QKP]) + _dot(lower, la_lo[r, :GLA_QKP])
        bcb = _dot(upper, la_hi[r, GLA_QKP:]) + _dot(upper, la_lo[r, GLA_QKP:])
        blf = bcf[C - 1:C, :]
        blb = bcb[0:1, :]
        qc = q[r]
        kc = k[r]
        qdf_ref[0, r, :] = (qc * jnp.exp2(bcf)).astype(BF16)
        kif_ref[0, r, :] = (kc * jnp.exp2(-bcf)).astype(BF16)
        kef_ref[0, r, :] = (kc * jnp.exp2(blf - bcf)).astype(BF16)
        qdb_ref[0, r, :] = (qc * jnp.exp2(bcb)).astype(BF16)
        kib_ref[0, r, :] = (kc * jnp.exp2(-bcb)).astype(BF16)
        keb_ref[0, r, :] = (kc * jnp.exp2(blb - bcb)).astype(BF16)
        decf_ref[0, c] = jnp.exp2(blf)
        decb_ref[0, c] = jnp.exp2(blb)

    vg = _dot(hb, w_ref[:, OFF_GV:OFF_Z])
    gv_ref[0] = vg[:, :GLA_V].astype(BF16)
    gg_ref[0] = vg[:, GLA_V:].astype(BF16)

    sqk = _dot(hb, w_ref[:, OFF_SQ:OFF_PU])
    ss = _dot((sqk * sqk).astype(BF16), ones_ref[...])
    sqk = sqk * lax.rsqrt(ss + EPS) * qkg_ref[...]
    tiles = []
    for t in range((SWA_Q + SWA_KV) // LANES):
        xt = sqk[:, t * LANES:(t + 1) * LANES]
        if rope:
            lane = lax.broadcasted_iota(jnp.int32, xt.shape, 1)
            first = (lane % (2 * ROPE_NF)) < ROPE_NF
            partner = jnp.where(first, pltpu.roll(xt, LANES - ROPE_NF, 1), pltpu.roll(xt, ROPE_NF, 1))
            xt = xt * cos_ref[...] + partner * sin_ref[...]
        tiles.append(xt.astype(BF16))
    for t in range(SWA_Q // LANES):
        sq_ref[0, :, t * LANES:(t + 1) * LANES] = tiles[t]
    skv_ref[0, :, :SWA_KV] = tiles[-1]

    pu_ref[0] = _dot(hb, w_ref[:, OFF_PU:IN_WP])


def _in_proj(x, mod, mod_row, g1, w, wdec, bdec, tri, ones64, qkg, cos, sin, *, tm, rope):
    B, T, D = x.shape
    nt = T // tm
    nc = T // GLA_CHUNK

    def tok(width, dtype):
        return (jax.ShapeDtypeStruct((B, T, width), dtype),
                pl.BlockSpec((1, tm, width), lambda i, b: (b, i, 0)))

    def const(arr):
        nd = arr.ndim
        return pl.BlockSpec(arr.shape, lambda i, b: (0,) * nd)

    dec = (jax.ShapeDtypeStruct((B, nc, 1, GLA_QKP), F32),
           pl.BlockSpec((1, tm // GLA_CHUNK, 1, GLA_QKP), lambda i, b: (b, i, 0, 0)))
    outs = [tok(GLA_QKP, BF16)] * 6 + [dec, dec] + [tok(GLA_V, BF16), tok(GLA_V, BF16),
                                                   tok(SWA_Q, BF16), tok(2 * SWA_KV, BF16),
                                                   tok(POOL_W, F32)]
    return pl.pallas_call(
        functools.partial(_in_proj_kernel, tm=tm, rope=rope),
        grid=(nt, B),
        in_specs=[pl.BlockSpec((1, tm, D), lambda i, b: (b, i, 0)),
                  pl.BlockSpec((1, N_MOD, D), lambda i, b: (mod_row(b), 0, 0)),
                  const(g1), const(w), const(wdec), const(bdec), const(tri), const(ones64), const(qkg),
                  pl.BlockSpec((tm, LANES), lambda i, b: (i, 0)),
                  pl.BlockSpec((tm, LANES), lambda i, b: (i, 0))],
        out_specs=[o[1] for o in outs],
        out_shape=[o[0] for o in outs],
        compiler_params=_params(),
        name="in_proj",
    )(x, mod, g1, w, wdec, bdec, tri, ones64, qkg, cos, sin)


def _gla_kernel(qdf_ref, kif_ref, kef_ref, vf_ref, decf_ref, qdb_ref, kib_ref, keb_ref, vb_ref, decb_ref,
                s0_ref, bd_ref, of_ref, ob_ref, sfin_ref, st_ref, *, tg):
    i = pl.program_id(1)
    nb = pl.num_programs(1)
    C = GLA_CHUNK
    ncb = tg // C

    @pl.when(i == 0)
    def _():
        st_ref[...] = s0_ref[0]

    HC = GLA_HEADS * C
    row_head = lax.broadcasted_iota(jnp.int32, (HC, GLA_QKP), 0) // C
    k_mask = row_head == lax.broadcasted_iota(jnp.int32, (HC, GLA_QKP), 1) // GLA_DKP
    vlane = lax.broadcasted_iota(jnp.int32, (HC, GLA_V), 1)
    vhead = sum((vlane >= h * GLA_DV).astype(jnp.int32) for h in range(1, GLA_HEADS))
    v_mask = lax.broadcasted_iota(jnp.int32, (HC, GLA_V), 0) // C == vhead
    row_i = lax.broadcasted_iota(jnp.int32, (C, HC), 0)
    col_j = lax.broadcasted_iota(jnp.int32, (C, HC), 1) % C
    bd = bd_ref[...]

    def chunk(d, qd_ref, ki_ref, ke_ref, v_ref, dec_ref, o_ref, c):
        r = slice(c * C, (c + 1) * C)
        qd = qd_ref[0, r, :]
        ki = ki_ref[0, r, :]
        ke = ke_ref[0, r, :]
        vv = v_ref[0, r, :]
        k_bd = jnp.where(k_mask, jnp.concatenate([ki] * GLA_HEADS, axis=0), jnp.zeros((), BF16))
        v_bd = jnp.where(v_mask, jnp.concatenate([vv] * GLA_HEADS, axis=0), jnp.zeros((), BF16))
        st = st_ref[d]
        rhs = jnp.concatenate([k_bd, st.astype(BF16)], axis=0)
        res = lax.dot_general(qd, rhs, _NT, preferred_element_type=F32)
        keep = (col_j <= row_i) if d == 0 else (col_j >= row_i)
        p = jnp.where(keep, res[:, :HC], 0.0).astype(BF16)
        o = res[:, HC:] + _dot(p, v_bd)
        o_ref[0, r, :] = o.astype(BF16)
        upd = lax.dot_general(vv, ke, _TN, preferred_element_type=F32)
        st_ref[d] = st * dec_ref[0, c] + upd * bd

    for c in range(ncb):
        chunk(0, qdf_ref, kif_ref, kef_ref, vf_ref, decf_ref, of_ref, c)
        chunk(1, qdb_ref, kib_ref, keb_ref, vb_ref, decb_ref, ob_ref, ncb - 1 - c)

    @pl.when(i == nb - 1)
    def _():
        sfin_ref[0] = st_ref[...]


def _gla(qdf, kif, kef, qdb, kib, keb, decf, decb, v, s0, bd, *, tg):
    B, T, _ = v.shape
    nb = T // tg
    ncb = tg // GLA_CHUNK

    def fwd(width):
        return pl.BlockSpec((1, tg, width), lambda b, i: (b, i, 0))

    def bwd(width):
        return pl.BlockSpec((1, tg, width), lambda b, i: (b, nb - 1 - i, 0))

    dec_f = pl.BlockSpec((1, ncb, 1, GLA_QKP), lambda b, i: (b, i, 0, 0))
    dec_b = pl.BlockSpec((1, ncb, 1, GLA_QKP), lambda b, i: (b, nb - 1 - i, 0, 0))
    st_spec = pl.BlockSpec((1, 2, GLA_V, GLA_QKP), lambda b, i: (b, 0, 0, 0))
    return pl.pallas_call(
        functools.partial(_gla_kernel, tg=tg),
        grid=(B, nb),
        in_specs=[fwd(GLA_QKP), fwd(GLA_QKP), fwd(GLA_QKP), fwd(GLA_V), dec_f,
                  bwd(GLA_QKP), bwd(GLA_QKP), bwd(GLA_QKP), bwd(GLA_V), dec_b,
                  st_spec, pl.BlockSpec(bd.shape, lambda b, i: (0, 0))],
        out_specs=[fwd(GLA_V), bwd(GLA_V), st_spec],
        out_shape=[jax.ShapeDtypeStruct((B, T, GLA_V), BF16),
                   jax.ShapeDtypeStruct((B, T, GLA_V), BF16),
                   jax.ShapeDtypeStruct((B, 2, GLA_V, GLA_QKP), F32)],
        scratch_shapes=[pltpu.VMEM((2, GLA_V, GLA_QKP), F32)],
        compiler_params=_params(),
        name="gla",
    )(qdf, kif, kef, v, decf, qdb, kib, keb, v, decb, s0, bd)


def _swa_kernel(*refs, n_local, n_sub, seq_len):
    if n_local:
        q_ref, *loc_refs, ckv_ref, sink_ref, o_ref = refs
    else:
        q_ref, ckv_ref, sink_ref, o_ref = refs
        loc_refs = []
    QB = Q_BLOCK
    lane = lax.broadcasted_iota(jnp.int32, (QB, LANES), 1)
    low = lane < HEAD_DIM
    n_loc = n_local * QB
    band = None
    if n_local:
        qi = lax.broadcasted_iota(jnp.int32, (QB, n_loc), 0)
        kj = lax.broadcasted_iota(jnp.int32, (QB, n_loc), 1)
        dist = kj - QB - qi
        band = jnp.where((dist >= -WINDOW) & (dist <= WINDOW), 0.0, NEG_INF)
    for sub in range(n_sub):
        _swa_block(pl.program_id(1) * n_sub + sub, sub, q_ref, loc_refs[sub:sub + n_local] + [ckv_ref],
                   sink_ref, o_ref, low, band, n_local=n_local, n_loc=n_loc, seq_len=seq_len)


def _swa_block(n, sub, q_ref, kv_parts, sink_ref, o_ref, low, band, *, n_local, n_loc, seq_len):
    QB = Q_BLOCK
    rows = slice(sub * QB, (sub + 1) * QB)
    kcat = jnp.concatenate([r[0, :, :SWA_KV] for r in kv_parts], axis=0)
    vcat = jnp.concatenate([r[0, :, SWA_KV:] for r in kv_parts], axis=0)
    if n_local:
        k_pos = (n - 1) * QB + lax.broadcasted_iota(jnp.int32, (1, n_loc), 1)
        bias = band + jnp.where((k_pos >= 0) & (k_pos < seq_len), 0.0, NEG_INF)

    def scores(g):
        slabs = []
        for t in range(SWA_GROUP):
            qt = q_ref[0, rows, t * LANES:(t + 1) * LANES]
            slabs.append(jnp.where(low if g == 0 else ~low, qt, jnp.zeros_like(qt)))
        return lax.dot_general(jnp.concatenate(slabs, axis=0), kcat, _NT,
                               preferred_element_type=F32)

    def attend(g, s_grp):
        ps, dens = [], []
        for t in range(SWA_GROUP):
            sink = sink_ref[g * SWA_GROUP + t] * LOG2E
            sh = s_grp[t * QB:(t + 1) * QB]
            s_ctx = sh[:, n_loc:]
            m = jnp.maximum(jnp.max(s_ctx, axis=-1, keepdims=True), sink)
            if n_local:
                s_loc = sh[:, :n_loc] + bias
                m = jnp.maximum(m, jnp.max(s_loc, axis=-1, keepdims=True))
                p_loc = jnp.exp2(s_loc - m)
            p_ctx = jnp.exp2(s_ctx - m)
            den = jnp.sum(p_ctx, axis=-1, keepdims=True) + jnp.exp2(sink - m)
            if n_local:
                den = den + jnp.sum(p_loc, axis=-1, keepdims=True)
                ps.append(jnp.concatenate([p_loc, p_ctx], axis=1).astype(BF16))
            else:
                ps.append(p_ctx.astype(BF16))
            dens.append(den)
        o_grp = _dot(jnp.concatenate(ps, axis=0), vcat)
        return [o_grp[t * QB:(t + 1) * QB] / dens[t] for t in range(SWA_GROUP)]

    s_groups = [scores(g) for g in range(SWA_KV_HEADS)]
    o_a = attend(0, s_groups[0])
    o_b = attend(1, s_groups[1])
    for t in range(SWA_GROUP):
        o_ref[0, rows, t * LANES:(t + 1) * LANES] = jnp.where(low, o_a[t], o_b[t]).astype(BF16)


def _swa(q, kv, ckv, sink, *, local):
    B, S, _ = q.shape
    L = ckv.shape[1]
    nq = S // Q_BLOCK
    n_sub = min(SWA_SUB, nq)
    qspec = pl.BlockSpec((1, n_sub * Q_BLOCK, SWA_Q), lambda b, n: (b, n, 0))
    cspec = pl.BlockSpec((1, L, 2 * SWA_KV), lambda b, n: (b, 0, 0))
    sspec = pl.BlockSpec(memory_space=pltpu.SMEM)
    if local:
        def blk(off):
            return pl.BlockSpec((1, Q_BLOCK, 2 * SWA_KV),
                                lambda b, n: (b, jnp.clip(n * n_sub + off, 0, nq - 1), 0))
        offs = range(-1, n_sub + 1)
        in_specs = [qspec] + [blk(off) for off in offs] + [cspec, sspec]
        args = (q,) + (kv,) * len(offs) + (ckv, sink)
    else:
        in_specs = [qspec, cspec, sspec]
        args = (q, ckv, sink)
    return pl.pallas_call(
        functools.partial(_swa_kernel, n_local=3 if local else 0, n_sub=n_sub, seq_len=S),
        grid=(B, nq // n_sub),
        in_specs=in_specs,
        out_specs=qspec,
        out_shape=jax.ShapeDtypeStruct((B, S, SWA_Q), BF16),
        compiler_params=_params(),
        name="swa",
    )(*args)


def _out_kernel(x_ref, of_ref, ob_ref, gg_ref, swa_ref, pu_ref, pup_ref, pun_ref, icnt_ref, mod_ref,
                gng_ref, ones_ref, poolw_ref, pools_ref, wout_ref, g2_ref,
                x1_ref, h2_ref, ext_ref, cen_ref, hs_ref, *, tm):
    i = pl.program_id(0)
    nt = pl.num_programs(0)
    HP = POOL_HALF_MAX

    o = of_ref[0].astype(F32) + ob_ref[0].astype(F32)
    ss = _dot((o * o).astype(BF16), ones_ref[...])
    gg = gg_ref[0].astype(F32)
    gla = o * lax.rsqrt(ss * (1.0 / GLA_DV) + EPS) * gng_ref[...] * (gg * _sigmoid(gg))

    ext_ref[0:HP, :] = jnp.where(i > 0, pup_ref[0], 0.0)
    ext_ref[HP:HP + tm, :] = pu_ref[0]
    ext_ref[HP + tm:, :] = jnp.where(i < nt - 1, pun_ref[0], 0.0)
    upper_group = lax.broadcasted_iota(jnp.int32, (POOL_ROWS, LANES), 1) >= POOL_GROUP
    for rb in range(tm // POOL_ROWS):
        r0 = HP + rb * POOL_ROWS
        rows = slice(rb * POOL_ROWS, (rb + 1) * POOL_ROWS)
        for tile in range(POOL_W // LANES):
            ls = slice(tile * LANES, (tile + 1) * LANES)
            n_rows = POOL_ROWS + 2 * HP
            run = ext_ref[r0 - HP:r0 + POOL_ROWS + HP, ls]
            wins = []
            span = 1
            for w in POOL_WINDOWS:
                while span < w:
                    run = run + pltpu.roll(run, span, 0)
                    span *= 2
                if w in POOL_WINDOWS[2 * tile:2 * tile + 2]:
                    lead = w // 2 - 1
                    aligned = pltpu.roll(run, n_rows - lead, 0) if lead else run
                    wins.append(aligned[HP:HP + POOL_ROWS])
                if len(wins) == 2:
                    break
            win = jnp.where(upper_group, wins[1], wins[0])
            centred = win * icnt_ref[rows, ls] - ext_ref[r0:r0 + POOL_ROWS, ls]
            cen_ref[rows, ls] = centred.astype(BF16)
    pool = _dot(cen_ref[...], poolw_ref[...]) * pools_ref[...]

    cat = jnp.concatenate([gla.astype(BF16), swa_ref[0], pool.astype(BF16)], axis=1)
    y = _dot(cat, wout_ref[...])
    g1 = mod_ref[0, 2:3, :]
    sh2 = mod_ref[0, 3:4, :]
    sc2 = mod_ref[0, 4:5, :]
    x1 = x_ref[0] + g1 * y
    x1_ref[0] = x1
    h2 = _rms(x1) * (g2_ref[...] * (1.0 + sc2)) + sh2
    J = tm // TOKEN_RESIDUES
    for lt in range(D_MODEL // LANES):
        ls = slice(lt * LANES, (lt + 1) * LANES)
        hs_ref[lt] = h2[:, ls]
        for k in range(TOKEN_RESIDUES):
            h2_ref[0, k * J:(k + 1) * J, ls] = hs_ref[lt, pl.ds(k, J, stride=TOKEN_RESIDUES), :].astype(BF16)


def _out_proj(x, o_f, o_b, gg, swa, pu, icnt, mod, mod_row, gng, ones96, poolw, pools, wout, g2, *, tm):
    B, T, D = x.shape
    nt = T // tm
    HP = POOL_HALF_MAX
    nh = T // HP

    def tok(width):
        return pl.BlockSpec((1, tm, width), lambda i, b: (b, i, 0))

    def const(arr):
        nd = arr.ndim
        return pl.BlockSpec(arr.shape, lambda i, b: (0,) * nd)

    prev = pl.BlockSpec((1, HP, POOL_W), lambda i, b: (b, jnp.maximum(i * (tm // HP) - 1, 0), 0))
    nxt = pl.BlockSpec((1, HP, POOL_W), lambda i, b: (b, jnp.minimum((i + 1) * (tm // HP), nh - 1), 0))
    return pl.pallas_call(
        functools.partial(_out_kernel, tm=tm),
        grid=(nt, B),
        in_specs=[tok(D), tok(GLA_V), tok(GLA_V), tok(GLA_V), tok(SWA_Q), tok(POOL_W), prev, nxt,
                  pl.BlockSpec((tm, POOL_W), lambda i, b: (i, 0)),
                  pl.BlockSpec((1, N_MOD, D), lambda i, b: (mod_row(b), 0, 0)),
                  const(gng), const(ones96), const(poolw), const(pools), const(wout), const(g2)],
        out_specs=[tok(D), tok(D)],
        out_shape=[jax.ShapeDtypeStruct((B, T, D), F32), jax.ShapeDtypeStruct((B, T, D), BF16)],
        scratch_shapes=[pltpu.VMEM((tm + 2 * HP, POOL_W), F32), pltpu.VMEM((tm, POOL_W), BF16),
                        pltpu.VMEM((D // LANES, tm, LANES), F32)],
        compiler_params=_params(),
        name="out_proj",
    )(x, o_f, o_b, gg, swa, pu, pu, pu, icnt, mod, gng, ones96, poolw, pools, wout, g2)


def _ffn_kernel(h_ref, hp_ref, hn_ref, x_ref, mod_ref, wup_ref, cwb_ref, wdn_ref,
                o_ref, hx_ref, u_ref, act_ref, nat_ref, *, tm):
    i = pl.program_id(0)
    nt = pl.num_programs(0)
    H = FFN_HALO
    N = N_FF_CHUNKS
    K = TOKEN_RESIDUES
    J = tm // K
    hx_ref[0:tm, :] = h_ref[0]
    before = jnp.where(i > 0, hp_ref[0].astype(F32)[H - 1:H], 0.0)
    after = jnp.where(i < nt - 1, hn_ref[0].astype(F32)[0:1], 0.0)
    halo_row = lax.broadcasted_iota(jnp.int32, (H, D_MODEL), 0)
    hx_ref[tm:, :] = jnp.where(halo_row == 0, before, jnp.where(halo_row == 1, after, 0.0)).astype(BF16)
    row8 = lax.broadcasted_iota(jnp.int32, (SUBLANES, LANES), 0)

    def up(fc, slot):
        for part in range(2):
            c0 = part * D_FF + fc * FF_CHUNK
            u_ref[slot, part] = _dot(hx_ref[...], wup_ref[:, c0:c0 + FF_CHUNK])

    def conv_act(fc, slot):
        for lt in range(FF_CHUNK // LANES):
            ls = slice(lt * LANES, (lt + 1) * LANES)
            wts = [cwb_ref[part, fc, :, ls] for part in range(2)]
            for k in range(K):
                taps = []
                for part in range(2):
                    u = u_ref.at[slot, part]
                    if k > 0:
                        prev = u[(k - 1) * J:k * J, ls]
                    else:
                        rolled = pltpu.roll(u[(K - 1) * J:K * J, ls], 1, 0)
                        head = jnp.where(row8 == 0, u[tm:tm + 1, ls], rolled[:SUBLANES])
                        prev = jnp.concatenate([head, rolled[SUBLANES:]], axis=0)
                    if k < K - 1:
                        nxt = u[(k + 1) * J:(k + 2) * J, ls]
                    else:
                        rolled = pltpu.roll(u[0:J, ls], J - 1, 0)
                        tail = jnp.where(row8 == SUBLANES - 1, u[tm + 1:tm + 2, ls], rolled[J - SUBLANES:])
                        nxt = jnp.concatenate([rolled[:J - SUBLANES], tail], axis=0)
                    w = wts[part]
                    taps.append(prev * w[0:1] + u[k * J:(k + 1) * J, ls] * w[1:2] + nxt * w[2:3] + w[3:4])
                a, g = taps
                act_ref[fc, k * J:(k + 1) * J, ls] = ((g * _sigmoid(g)) * a).astype(BF16)

    up(0, 0)
    for s in range(1, N + 1):
        if s < N:
            up(s, s % 2)
        conv_act(s - 1, (s - 1) % 2)
    act = jnp.concatenate([act_ref[c] for c in range(N)], axis=1)
    y = mod_ref[0, 5:6, :] * _dot(act, wdn_ref[...])
    for lt in range(D_MODEL // LANES):
        ls = slice(lt * LANES, (lt + 1) * LANES)
        for k in range(K):
            nat_ref[lt, pl.ds(k, J, stride=K), :] = y[k * J:(k + 1) * J, ls]
        o_ref[0, :, ls] = x_ref[0, :, ls] + nat_ref[lt]


def _ffn(h2, x1, mod, mod_row, wup, cwb, wdn, *, tm):
    B, T, D = x1.shape
    nt = T // tm
    H = FFN_HALO
    nh = T // H

    def tok(width):
        return pl.BlockSpec((1, tm, width), lambda i, b: (b, i, 0))

    def const(arr):
        nd = arr.ndim
        return pl.BlockSpec(arr.shape, lambda i, b: (0,) * nd)

    prev = pl.BlockSpec((1, H, D), lambda i, b: (b, jnp.maximum(i * (tm // H) - 1, 0), 0))
    nxt = pl.BlockSpec((1, H, D), lambda i, b: (b, jnp.minimum((i + 1) * (tm // H), nh - 1), 0))
    return pl.pallas_call(
        functools.partial(_ffn_kernel, tm=tm),
        grid=(nt, B),
        in_specs=[tok(D), prev, nxt, tok(D),
                  pl.BlockSpec((1, N_MOD, D), lambda i, b: (mod_row(b), 0, 0)),
                  const(wup), const(cwb), const(wdn)],
        out_specs=tok(D),
        out_shape=jax.ShapeDtypeStruct((B, T, D), F32),
        scratch_shapes=[pltpu.VMEM((tm + H, D), BF16),
                        pltpu.VMEM((2, 2, tm + H, FF_CHUNK), F32),
                        pltpu.VMEM((N_FF_CHUNKS, tm, FF_CHUNK), BF16),
                        pltpu.VMEM((D // LANES, tm, LANES), F32)],
        compiler_params=_params(),
        name="ffn",
    )(h2, h2, h2, x1, mod, wup, cwb, wdn)


def _pad_heads(a, n_heads, width, padded):
    a = a.reshape(a.shape[:-1] + (n_heads, width))
    a = jnp.pad(a, [(0, 0)] * (a.ndim - 1) + [(0, padded - width)])
    return a.reshape(a.shape[:-2] + (n_heads * padded,))


def _permute_heads(a, axis):
    shape = a.shape
    axis = axis % a.ndim
    a = a.reshape(shape[:axis] + (SWA_HEADS, HEAD_DIM) + shape[axis + 1:])
    a = jnp.take(a, jnp.array(SWA_HEAD_ORDER), axis=axis)
    return a.reshape(shape)


def _block_diag_ones(n, block):
    idx = np.arange(n) // block
    return jnp.asarray(idx[:, None] == idx[None, :], BF16)


def _pack_params(w_in, gla_w_dec, gla_b_dec, q_norm_g, k_norm_g, pool_w, w_out, conv_w, conv_b):
    depth = w_in.shape[0]
    gq, gk, gv, gg, zf, zb, aq, ak, av, pu = jnp.split(
        w_in, np.cumsum([GLA_HEADS * GLA_DK] * 2 + [GLA_V] * 2 + [GLA_RANK] * 2
                        + [SWA_Q, SWA_KV, SWA_KV]).tolist(), axis=-1)
    z = jnp.pad(jnp.concatenate([zf, zb], axis=-1), ((0, 0), (0, 0), (0, Z_PAD - 2 * GLA_RANK)))
    w = jnp.concatenate([_pad_heads(gq, GLA_HEADS, GLA_DK, GLA_DKP),
                         _pad_heads(gk, GLA_HEADS, GLA_DK, GLA_DKP),
                         gv, gg, z, av, _permute_heads(aq, -1), ak, pu], axis=-1).astype(BF16)
    dec = _pad_heads(gla_w_dec, GLA_HEADS, GLA_DK, GLA_DKP)
    none = jnp.zeros_like(dec[:, 0])
    wdec = jnp.concatenate([jnp.concatenate([dec[:, 0], none], axis=-1),
                            jnp.concatenate([none, dec[:, 1]], axis=-1)], axis=-2)
    wdec = jnp.pad(wdec, ((0, 0), (0, Z_PAD - 2 * GLA_RANK), (0, 0))).astype(BF16)
    bdec = _pad_heads(gla_b_dec, GLA_HEADS, GLA_DK, GLA_DKP).reshape(depth, 1, 2 * GLA_QKP)
    qkg = jnp.concatenate([jnp.tile(q_norm_g, (1, SWA_HEADS)) * (HEAD_DIM ** -0.5 * LOG2E),
                           jnp.tile(k_norm_g, (1, SWA_KV_HEADS))], axis=-1)[:, None, :]
    n_groups = len(POOL_WINDOWS)
    poolw = jnp.concatenate(
        [jnp.pad(pool_w[:, g], ((0, 0), (0, 0), (g * POOL_GROUP, (n_groups - 1 - g) * POOL_GROUP)))
         for g in range(n_groups)], axis=-2).astype(BF16)
    wout = jnp.concatenate([w_out[:, :GLA_V], _permute_heads(w_out[:, GLA_V:GLA_V + SWA_Q], -2),
                            w_out[:, GLA_V + SWA_Q:]], axis=-2).astype(BF16)
    cwb = jnp.concatenate([conv_w, conv_b[:, None, :]], axis=-2).reshape(depth, 4, 2, N_FF_CHUNKS, FF_CHUNK)
    return dict(w=w, wdec=wdec, bdec=bdec, qkg=qkg, poolw=poolw, wout=wout,
                cwb=jnp.transpose(cwb, (0, 2, 3, 1, 4)))


def _rope_tables(n_tokens):
    rows_n = n_tokens // GRID_W
    rows = jnp.repeat(jnp.arange(rows_n), GRID_W).astype(F32)
    cols = jnp.tile(jnp.arange(GRID_W), rows_n).astype(F32)
    nf = ROPE_NF
    inv = ROPE_BASE ** (-jnp.arange(nf, dtype=F32) / nf)
    ar = rows[:, None] * inv
    ac = cols[:, None] * inv
    cos = jnp.concatenate([jnp.cos(ar), jnp.cos(ar), jnp.cos(ac), jnp.cos(ac)], axis=-1)
    sin = jnp.concatenate([-jnp.sin(ar), jnp.sin(ar), -jnp.sin(ac), jnp.sin(ac)], axis=-1)
    return jnp.tile(cos, (1, LANES // HEAD_DIM)), jnp.tile(sin, (1, LANES // HEAD_DIM))


def _pool_inv_counts(n_tokens):
    t = jnp.arange(n_tokens)[:, None]
    half = jnp.repeat(jnp.asarray(POOL_WINDOWS) // 2, POOL_GROUP)[None, :]
    cnt = jnp.minimum(t + half, n_tokens) - jnp.maximum(t - half, 0)
    return 1.0 / cnt.astype(F32)


def _tile_sizes(T):
    return dict(ti=min(T, 1024), tg=min(T, 1024), tm=min(T, 512))


def kernel(x, c, ctx, c_ctx, w_ada, b_ada, norm1_g, w_in, gla_w_dec, gla_b_dec, gla_norm_g, q_norm_g,
           k_norm_g, sink_logit, pool_w, pool_scale, w_out, norm2_g, w_up, conv_w, conv_b, w_down):
    B, S, D = x.shape
    L = ctx.shape[1]
    assert B + 1 <= MOD_ROWS and D == D_MODEL
    c_rows = jnp.zeros((MOD_ROWS, D), F32).at[:B].set(c).at[B].set(c_ctx)
    mod_all = _modulation(c_rows, w_ada, b_ada).reshape(DEPTH, MOD_ROWS, N_MOD, D)
    cos, sin = _rope_tables(S)
    chunk_ones = np.ones((GLA_CHUNK, GLA_CHUNK), np.float32)
    tri = jnp.asarray(np.stack([np.tril(chunk_ones), np.triu(chunk_ones)]), BF16)
    ones64 = _block_diag_ones(SWA_Q + SWA_KV, HEAD_DIM) * jnp.asarray(1.0 / HEAD_DIM, BF16)
    ones96 = _block_diag_ones(GLA_V, GLA_DV)
    icnt_lat, icnt_ctx = _pool_inv_counts(S), _pool_inv_counts(L)
    bd = jnp.asarray((np.arange(GLA_V) // GLA_DV)[:, None] == (np.arange(GLA_QKP) // GLA_DKP)[None, :], F32)
    lat, cx = _tile_sizes(S), _tile_sizes(L)
    lat_row = lambda b: b
    ctx_row = lambda b: B
    zero_state = jnp.zeros((B, 2, GLA_V, GLA_QKP), F32)

    packed = _pack_params(w_in, gla_w_dec, gla_b_dec, q_norm_g, k_norm_g, pool_w, w_out, conv_w, conv_b)
    w_up_bf, w_down_bf = w_up.astype(BF16), w_down.astype(BF16)

    for l in range(DEPTH):
        p = {name: arr[l] for name, arr in packed.items()}
        mod = mod_all[l]
        g1 = norm1_g[l][None, :]
        g2 = norm2_g[l][None, :]
        gng = jnp.tile(gla_norm_g[l], GLA_HEADS)[None, :]
        pools = pool_scale[l][None, :]
        wup = w_up_bf[l]
        wdn = w_down_bf[l]
        shared = (g1, p["w"], p["wdec"], p["bdec"], tri, ones64, p["qkg"])
        update_ctx = l < DEPTH - 1

        (cqdf, ckif, ckef, cqdb, ckib, ckeb, cdecf, cdecb, cgv, cgg, csq, cskv, cpu) = _in_proj(
            ctx, mod, ctx_row, *shared, cos[:L], sin[:L], tm=cx["ti"], rope=False)
        co_f, co_b, st = _gla(cqdf, ckif, ckef, cqdb, ckib, ckeb, cdecf, cdecb, cgv, zero_state, bd,
                              tg=cx["tg"])

        (qdf, kif, kef, qdb, kib, keb, decf, decb, gv, gg, sq, skv, pu) = _in_proj(
            x, mod, lat_row, *shared, cos, sin, tm=lat["ti"], rope=True)
        o_f, o_b, _ = _gla(qdf, kif, kef, qdb, kib, keb, decf, decb, gv, st, bd, tg=lat["tg"])
        swa = _swa(sq, skv, cskv, sink_logit[l], local=True)
        x1, h2 = _out_proj(x, o_f, o_b, gg, swa, pu, icnt_lat, mod, lat_row, gng, ones96, p["poolw"],
                           pools, p["wout"], g2, tm=lat["tm"])
        x = _ffn(h2, x1, mod, lat_row, wup, p["cwb"], wdn, tm=lat["tm"])

        if update_ctx:
            cswa = _swa(csq, None, cskv, sink_logit[l], local=False)
            c1, ch2 = _out_proj(ctx, co_f, co_b, cgg, cswa, cpu, icnt_ctx, mod, ctx_row, gng, ones96,
                                p["poolw"], pools, p["wout"], g2, tm=cx["tm"])
            ctx = _ffn(ch2, c1, mod, ctx_row, wup, p["cwb"], wdn, tm=cx["tm"])
    return x
```
